```python
import jax, jax.numpy as jnp
from jax import lax
import numpy as np

D_MODEL = 1024
BATCH = 8
SEQ = 8192
DEPTH = 4
DEC_BATCH = 32
DEC_SEQ = 64
PAST_LEN = 1024

CHUNK = 64
N_GROUPS = 4
D_MIX = D_MODEL
GROUP_W = D_MIX // N_GROUPS
CONV_W = 31
DN_HEADS = 4
DN_HEAD_DIM = GROUP_W // DN_HEADS
DN_CONV = 4
POOL_WINDOWS = (2, 4, 8, 16)
POOL_GROUP = GROUP_W // 4
POOL_PREFIX = 15
SB_HEADS = 4
SB_HEAD_DIM = GROUP_W // SB_HEADS
SB_BLOCK = 128
SB_NEG = -1e30
D_FF = ((8 * D_MODEL // 3 + 255) // 256) * 256
N_MOD = 6
EPS = 1e-6
IN_CONV = 2 * GROUP_W
IN_DN_QKV = 3 * GROUP_W
IN_DN_GATE = GROUP_W
IN_DN_A = DN_HEADS
IN_DN_B = DN_HEADS
IN_POOL = GROUP_W
IN_SB = 3 * GROUP_W
D_IN = IN_CONV + IN_DN_QKV + IN_DN_GATE + IN_DN_A + IN_DN_B + IN_POOL + IN_SB
IN_SPLITS = (IN_CONV,
             IN_CONV + IN_DN_QKV,
             IN_CONV + IN_DN_QKV + IN_DN_GATE,
             IN_CONV + IN_DN_QKV + IN_DN_GATE + IN_DN_A,
             IN_CONV + IN_DN_QKV + IN_DN_GATE + IN_DN_A + IN_DN_B,
             IN_CONV + IN_DN_QKV + IN_DN_GATE + IN_DN_A + IN_DN_B + IN_POOL)

kernel_name = 'hybrid_streaming_encoder_step'


def rms_norm(x, g):
    xf = x.astype(jnp.float32)
    y = xf * lax.rsqrt(jnp.mean(xf * xf, axis=-1, keepdims=True) + EPS)
    return (y * g.astype(jnp.float32)).astype(x.dtype)


def l2_norm(x):
    return x * lax.rsqrt(jnp.sum(x * x, axis=-1, keepdims=True) + EPS)


def causal_depthwise(full, w):
    return lax.conv_general_dilated(full, w.astype(full.dtype)[:, None, :], window_strides=(1,),
                                    padding='VALID', dimension_numbers=('NWC', 'WIO', 'NWC'),
                                    feature_group_count=full.shape[-1])


def conformer_conv(u, prev, w_dw, b_dw, ln_g, ln_b):
    a = u[..., :GROUP_W] * jax.nn.sigmoid(u[..., GROUP_W:])
    full = jnp.concatenate([prev.astype(a.dtype), a], axis=1)
    y = (causal_depthwise(full, w_dw) + b_dw.astype(a.dtype)).astype(jnp.float32)
    mu = jnp.mean(y, axis=-1, keepdims=True)
    var = jnp.mean(jnp.square(y - mu), axis=-1, keepdims=True)
    yn = (y - mu) * lax.rsqrt(var + EPS) * ln_g.astype(jnp.float32) + ln_b.astype(jnp.float32)
    return jax.nn.silu(yn).astype(u.dtype), full[:, -(CONV_W - 1):]


def chunk_gated_delta(q, k, v, g, beta, S0):
    B, L, H, Dk = q.shape
    Dv = v.shape[-1]
    pad = (-L) % CHUNK
    if pad:
        padf = lambda t: jnp.pad(t, [(0, 0), (0, pad)] + [(0, 0)] * (t.ndim - 2))
        q, k, v, g, beta = padf(q), padf(k), padf(v), padf(g), padf(beta)
    Lp = L + pad
    N = Lp // CHUNK

    def chunks(t):
        t = t.reshape((B, N, CHUNK, H) + t.shape[3:])
        return jnp.moveaxis(t, (1, 3), (0, 2))

    qc, kc, vc, bc = chunks(q), chunks(k), chunks(v), chunks(beta)
    gc = jnp.cumsum(chunks(g), axis=-1)
    idx = jnp.arange(CHUNK)
    causal = idx[:, None] >= idx[None, :]
    strict = idx[:, None] > idx[None, :]
    decay = jnp.exp(jnp.where(causal, gc[..., :, None] - gc[..., None, :], -jnp.inf))
    kb = kc * bc[..., None]
    lmat = jnp.where(strict, jnp.einsum('nbhid,nbhjd->nbhij', kb, kc) * decay, 0.0)
    rhs = jnp.concatenate([vc * bc[..., None], kb * jnp.exp(gc)[..., None]], axis=-1)
    sol = lax.linalg.triangular_solve(lmat + jnp.eye(CHUNK, dtype=jnp.float32), rhs,
                                      left_side=True, lower=True, unit_diagonal=True)
    u_c, w_c = sol[..., :Dv], sol[..., Dv:]
    attn = jnp.einsum('nbhid,nbhjd->nbhij', qc, kc) * decay
    q_dec = qc * jnp.exp(gc)[..., None]
    k_upd = kc * jnp.exp(gc[..., -1:] - gc)[..., None]
    g_last = jnp.exp(gc[..., -1])

    def step(S, xs):
        u_i, w_i, attn_i, q_i, k_i, gl = xs
        v_new = u_i - jnp.einsum('bhcd,bhde->bhce', w_i, S)
        o = jnp.einsum('bhcd,bhde->bhce', q_i, S) + jnp.einsum('bhij,bhje->bhie', attn_i, v_new)
        S = S * gl[..., None, None] + jnp.einsum('bhcd,bhce->bhde', k_i, v_new)
        return S, o

    S, o = lax.scan(step, S0, (u_c, w_c, attn, q_dec, k_upd, g_last))
    o = jnp.moveaxis(o, (0, 2), (1, 3)).reshape(B, Lp, H, Dv)[:, :L]
    return o, S


def gated_deltanet(u_qkv, u_gate, u_a, u_b, conv_prev, S0, conv_w, a_log, dt_bias, norm_g):
    B, L, _ = u_qkv.shape
    full = jnp.concatenate([conv_prev.astype(u_qkv.dtype), u_qkv], axis=1)
    qkv = jax.nn.silu(causal_depthwise(full, conv_w)).astype(jnp.float32)
    qkv = qkv.reshape(B, L, 3, DN_HEADS, DN_HEAD_DIM)
    q = l2_norm(qkv[:, :, 0]) * (DN_HEAD_DIM ** -0.5)
    k = l2_norm(qkv[:, :, 1])
    v = qkv[:, :, 2]
    beta = jax.nn.sigmoid(u_b.astype(jnp.float32))
    g = -jnp.exp(a_log.astype(jnp.float32)) * jax.nn.softplus(u_a.astype(jnp.float32) + dt_bias.astype(jnp.float32))
    o, S = chunk_gated_delta(q, k, v, g, beta, S0.astype(jnp.float32))
    gate = jax.nn.silu(u_gate.astype(jnp.float32).reshape(B, L, DN_HEADS, DN_HEAD_DIM))
    o = rms_norm(o, norm_g) * gate
    return o.reshape(B, L, GROUP_W).astype(u_qkv.dtype), S, full[:, -(DN_CONV - 1):]


def multiscale_pool(u, prev, pos0, w_pool, scale):
    B, L, C = u.shape
    P = POOL_PREFIX
    full = jnp.concatenate([prev.astype(u.dtype), u], axis=1)
    ff = full.astype(jnp.float32)
    cs = jnp.concatenate([jnp.zeros((B, 1, C), jnp.float32), jnp.cumsum(ff, axis=1)], axis=1)
    cur = ff[:, P:]
    pos = pos0 + jnp.arange(L)
    groups = []
    for gi, w in enumerate(POOL_WINDOWS):
        lo_c, hi_c = gi * POOL_GROUP, (gi + 1) * POOL_GROUP
        win = cs[:, P + 1:P + 1 + L, lo_c:hi_c] - cs[:, P + 1 - w:P + 1 - w + L, lo_c:hi_c]
        cnt = jnp.minimum(pos + 1, w).astype(jnp.float32)
        groups.append(win / cnt[None, :, None] - cur[:, :, lo_c:hi_c])
    pooled = jnp.stack(groups, axis=2)
    y = jnp.einsum('blgc,gcd->blgd', pooled, w_pool.astype(jnp.float32)).reshape(B, L, C)
    y = y * scale.astype(jnp.float32)
    return y.astype(u.dtype), full[:, -P:]


def stick_breaking(q, k, v, q_off):
    B, H, Lq, D = q.shape
    Lk = k.shape[2]
    KB = SB_BLOCK
    QB = min(SB_BLOCK, Lq)
    nb = Lq // QB
    qf = q.astype(jnp.float32) * (D ** -0.5)
    kf = k.astype(jnp.float32)
    vf = v.astype(jnp.float32)
    kidx = jnp.arange(KB)
    upper = (kidx[:, None] >= kidx[None, :]).astype(jnp.float32)
    outs = []
    for i in range(nb):
        q0 = q_off + i * QB
        n = max(KB, -(-(q0 + QB - 1) // KB) * KB)
        ki, vi = kf[:, :, :n], vf[:, :, :n]
        if n > Lk:
            padk = [(0, 0), (0, 0), (0, n - Lk), (0, 0)]
            ki, vi = jnp.pad(ki, padk), jnp.pad(vi, padk)
        nk = n // KB
        ki = ki.reshape(B, H, nk, KB, D)
        vi = vi.reshape(B, H, nk, KB, D)
        qpos = q0 + jnp.arange(QB)
        kpos = jnp.arange(n).reshape(nk, KB)
        mask = kpos[None] < qpos[:, None, None]
        z = jnp.where(mask, jnp.einsum('bhqd,bhnkd->bhqnk', qf[:, :, i * QB:(i + 1) * QB], ki), SB_NEG)
        sp = jax.nn.softplus(z)
        c_in = jnp.einsum('bhqnj,js->bhqns', sp, upper)
        tot = c_in[..., 0]
        later = jnp.flip(jnp.cumsum(jnp.flip(tot, -1), axis=-1), -1) - tot
        a = jnp.exp(z - c_in - later[..., None])
        outs.append(jnp.einsum('bhqnk,bhnkd->bhqd', a, vi))
    return jnp.concatenate(outs, axis=2)


def stick_breaking_mixer(u, k_prev, v_prev, qn_g, kn_g):
    B, L, _ = u.shape
    qkv = u.reshape(B, L, 3, SB_HEADS, SB_HEAD_DIM).transpose(2, 0, 3, 1, 4)
    q = rms_norm(qkv[0], qn_g)
    k = rms_norm(qkv[1], kn_g)
    v = qkv[2]
    k_all = jnp.concatenate([k_prev.astype(k.dtype), k], axis=2)
    v_all = jnp.concatenate([v_prev.astype(v.dtype), v], axis=2)
    o = stick_breaking(q, k_all, v_all, k_prev.shape[2])
    o = o.transpose(0, 2, 1, 3).reshape(B, L, GROUP_W).astype(u.dtype)
    return o, k, v


def trunk_layer(x, c, pos0, conv_prev, dn_S0, dn_conv_prev, pool_prev, k_prev, v_prev,
                w_ada, b_ada, norm_mix, norm_ffn, w_in, w_out, conv_dw_w, conv_dw_b, conv_ln_g, conv_ln_b,
                dn_conv_w, dn_a_log, dn_dt_bias, dn_norm_g, pool_w, pool_scale, sb_q_norm, sb_k_norm,
                ffn_w_gate, ffn_w_up, ffn_w_down):
    B = x.shape[0]
    mod = (jax.nn.silu(c) @ w_ada + b_ada).reshape(B, N_MOD, 1, D_MODEL)
    shift1, scale1, gate1 = mod[:, 0], mod[:, 1], mod[:, 2]
    shift2, scale2, gate2 = mod[:, 3], mod[:, 4], mod[:, 5]
    h = rms_norm(x, norm_mix) * (1 + scale1) + shift1
    u = h @ w_in
    u_conv, u_dn_qkv, u_dn_gate, u_dn_a, u_dn_b, u_pool, u_sb = jnp.split(u, IN_SPLITS, axis=-1)
    y_conv, conv_new = conformer_conv(u_conv, conv_prev, conv_dw_w, conv_dw_b, conv_ln_g, conv_ln_b)
    y_dn, S_new, dn_conv_new = gated_deltanet(u_dn_qkv, u_dn_gate, u_dn_a, u_dn_b, dn_conv_prev, dn_S0,
                                              dn_conv_w, dn_a_log, dn_dt_bias, dn_norm_g)
    y_pool, pool_new = multiscale_pool(u_pool, pool_prev, pos0, pool_w, pool_scale)
    y_sb, k_new, v_new = stick_breaking_mixer(u_sb, k_prev, v_prev, sb_q_norm, sb_k_norm)
    mix = jnp.concatenate([y_conv, y_dn, y_pool, y_sb], axis=-1) @ w_out
    x = x + gate1 * mix
    h2 = rms_norm(x, norm_ffn) * (1 + scale2) + shift2
    ffn = (jax.nn.silu(h2 @ ffn_w_gate) * (h2 @ ffn_w_up)) @ ffn_w_down
    x = x + gate2 * ffn
    return x, conv_new, S_new, dn_conv_new, pool_new, k_new, v_new


def setup_inputs(seed: int = 0) -> dict:
    key = jax.random.key(seed)
    ks = jax.random.split(key, 32)
    f32 = jnp.float32
    nrm = lambda k, shape, s: jax.random.normal(k, shape, f32) * s
    dt = jnp.exp(jax.random.uniform(ks[21], (DEPTH, DN_HEADS), f32, np.log(0.001), np.log(0.1)))
    return {
        'x_prompt': nrm(ks[0], (BATCH, SEQ, D_MODEL), 1.0),
        'x_sample': nrm(ks[1], (DEC_BATCH, DEC_SEQ, D_MODEL), 1.0),
        'c_prompt': nrm(ks[2], (BATCH, D_MODEL), 1.0),
        'c_sample': nrm(ks[3], (DEC_BATCH, D_MODEL), 1.0),
        'cache_conv': nrm(ks[4], (DEPTH, DEC_BATCH, CONV_W - 1, GROUP_W), 0.5),
        'state_dn': nrm(ks[5], (DEPTH, DEC_BATCH, DN_HEADS, DN_HEAD_DIM, DN_HEAD_DIM), 0.1),
        'cache_dn_conv': nrm(ks[6], (DEPTH, DEC_BATCH, DN_CONV - 1, 3 * GROUP_W), 1.0),
        'cache_pool': nrm(ks[7], (DEPTH, DEC_BATCH, POOL_PREFIX, GROUP_W), 1.0),
        'cache_sb_k': nrm(ks[8], (DEPTH, DEC_BATCH, SB_HEADS, PAST_LEN, SB_HEAD_DIM), 1.0),
        'cache_sb_v': nrm(ks[9], (DEPTH, DEC_BATCH, SB_HEADS, PAST_LEN, SB_HEAD_DIM), 1.0),
        'w_ada': nrm(ks[10], (DEPTH, D_MODEL, N_MOD * D_MODEL), 0.5 * D_MODEL ** -0.5),
        'b_ada': nrm(ks[11], (DEPTH, N_MOD * D_MODEL), 0.01),
        'norm_mix': 1.0 + nrm(ks[12], (DEPTH, D_MODEL), 0.05),
        'norm_ffn': 1.0 + nrm(ks[13], (DEPTH, D_MODEL), 0.05),
        'w_in': nrm(ks[14], (DEPTH, D_MODEL, D_IN), D_MODEL ** -0.5),
        'w_out': nrm(ks[15], (DEPTH, D_MIX, D_MODEL), D_MIX ** -0.5),
        'conv_dw_w': nrm(ks[16], (DEPTH, CONV_W, GROUP_W), CONV_W ** -0.5),
        'conv_dw_b': nrm(ks[17], (DEPTH, GROUP_W), 0.01),
        'conv_ln_g': 1.0 + nrm(ks[18], (DEPTH, GROUP_W), 0.05),
        'conv_ln_b': nrm(ks[19], (DEPTH, GROUP_W), 0.01),
        'dn_conv_w': nrm(ks[20], (DEPTH, DN_CONV, 3 * GROUP_W), DN_CONV ** -0.5),
        'dn_a_log': jnp.log(jax.random.uniform(ks[22], (DEPTH, DN_HEADS), f32, 1.0, 16.0)),
        'dn_dt_bias': dt + jnp.log(-jnp.expm1(-dt)),
        'dn_norm_g': 1.0 + nrm(ks[23], (DEPTH, DN_HEAD_DIM), 0.05),
        'pool_w': nrm(ks[24], (DEPTH, 4, POOL_GROUP, POOL_GROUP), POOL_GROUP ** -0.5),
        'pool_scale': 1.0 + nrm(ks[25], (DEPTH, GROUP_W), 0.1),
        'sb_q_norm': 1.0 + nrm(ks[26], (DEPTH, SB_HEAD_DIM), 0.05),
        'sb_k_norm': 1.0 + nrm(ks[27], (DEPTH, SB_HEAD_DIM), 0.05),
        'ffn_w_gate': nrm(ks[28], (DEPTH, D_MODEL, D_FF), D_MODEL ** -0.5),
        'ffn_w_up': nrm(ks[29], (DEPTH, D_MODEL, D_FF), D_MODEL ** -0.5),
        'ffn_w_down': nrm(ks[30], (DEPTH, D_FF, D_MODEL), D_FF ** -0.5),
    }


def reference(x_prompt, x_sample, c_prompt, c_sample, cache_conv, state_dn, cache_dn_conv, cache_pool,
              cache_sb_k, cache_sb_v, w_ada, b_ada, norm_mix, norm_ffn, w_in, w_out, conv_dw_w, conv_dw_b,
              conv_ln_g, conv_ln_b, dn_conv_w, dn_a_log, dn_dt_bias, dn_norm_g, pool_w, pool_scale,
              sb_q_norm, sb_k_norm, ffn_w_gate, ffn_w_up, ffn_w_down):
    Bp = x_prompt.shape[0]
    dt = x_prompt.dtype
    past = cache_sb_k.shape[3]
    xp, xs = x_prompt, x_sample
    new_p = [[] for _ in range(6)]
    new_s = [[] for _ in range(6)]
    for l in range(DEPTH):
        lw = (w_ada[l], b_ada[l], norm_mix[l], norm_ffn[l], w_in[l], w_out[l], conv_dw_w[l], conv_dw_b[l],
              conv_ln_g[l], conv_ln_b[l], dn_conv_w[l], dn_a_log[l], dn_dt_bias[l], dn_norm_g[l],
              pool_w[l], pool_scale[l], sb_q_norm[l], sb_k_norm[l], ffn_w_gate[l], ffn_w_up[l], ffn_w_down[l])
        xp, *sp = trunk_layer(xp, c_prompt, 0,
                              jnp.zeros((Bp, CONV_W - 1, GROUP_W), dt),
                              jnp.zeros((Bp, DN_HEADS, DN_HEAD_DIM, DN_HEAD_DIM), jnp.float32),
                              jnp.zeros((Bp, DN_CONV - 1, 3 * GROUP_W), dt),
                              jnp.zeros((Bp, POOL_PREFIX, GROUP_W), dt),
                              jnp.zeros((Bp, SB_HEADS, 0, SB_HEAD_DIM), dt),
                              jnp.zeros((Bp, SB_HEADS, 0, SB_HEAD_DIM), dt),
                              *lw)
        xs, *ss = trunk_layer(xs, c_sample, past, cache_conv[l], state_dn[l], cache_dn_conv[l], cache_pool[l],
                              cache_sb_k[l], cache_sb_v[l], *lw)
        for i in range(6):
            new_p[i].append(sp[i])
            new_s[i].append(ss[i])
    conv_p, dn_p, dnconv_p, pool_p, sbk_p, sbv_p = [jnp.stack(a) for a in new_p]
    conv_s, dn_s, dnconv_s, pool_s, sbk_s, sbv_s = [jnp.stack(a) for a in new_s]
    return (xp, xs, conv_p, conv_s, dn_p, dn_s, dnconv_p, dnconv_s, pool_p, pool_s, sbk_p, sbk_s, sbv_p, sbv_s)
```

```python
import functools

import jax
import jax.numpy as jnp
from jax import lax
from jax.experimental import pallas as pl
from jax.experimental.pallas import tpu as pltpu

D_MODEL = 1024
GROUP_W = 256
CONV_W = 31
DN_HEADS = 4
DN_HEAD_DIM = 64
DN_CONV = 4
CHUNK = 64
POOL_PREFIX = 15
SB_HEADS = 4
SB_HEAD_DIM = 64
SB_BLOCK = 128
SB_NEG = -1e30
D_FF = 2816
N_MOD = 6
EPS = 1e-6

V7X_LANES = 128
V7X_SUBLANES = 8
V7X_VMEM_BYTES = 64 * 1024 * 1024
VMEM_LIMIT = V7X_VMEM_BYTES * 7 // 8

ROW_TILE = 512
FF_SPLIT = 2
CONV_PAD = 32
DN_PAD = 8
POOL_PAD = 16
AB_W = V7X_LANES
IN_WIDTHS = (2 * GROUP_W, 3 * GROUP_W, GROUP_W, AB_W, GROUP_W, 3 * GROUP_W)

F32 = jnp.float32
BF16 = jnp.bfloat16
HI = lax.Precision.HIGHEST


def _cparams(*sem):
    return pltpu.CompilerParams(dimension_semantics=sem, vmem_limit_bytes=VMEM_LIMIT)


def _dot(a, b, precision=None):
    return jnp.dot(a, b, preferred_element_type=F32, precision=precision)


def _dot_nt(a, b, precision=None):
    return lax.dot_general(a, b, (((1,), (1,)), ((), ())), preferred_element_type=F32, precision=precision)


def _dot_tn(a, b, precision=None):
    return lax.dot_general(a, b, (((0,), (0,)), ((), ())), preferred_element_type=F32, precision=precision)


def _sigmoid(x):
    return 1.0 / (1.0 + jnp.exp(-x))


def _softplus(x):
    return jnp.maximum(x, 0.0) + jnp.log1p(jnp.exp(-jnp.abs(x)))


def _row_tiles(batch, length):
    if length >= ROW_TILE:
        assert length % ROW_TILE == 0
        return 1, ROW_TILE
    tb = max(1, min(batch, ROW_TILE // length))
    while batch % tb:
        tb -= 1
    return tb, length


def _const_spec(shape, layer=None):
    nd = len(shape)
    if layer is None:
        return pl.BlockSpec(shape, lambda *_: (0,) * nd, pipeline_mode=pl.Buffered(1))
    return pl.BlockSpec((None,) + shape, lambda *_: (layer,) + (0,) * nd, pipeline_mode=pl.Buffered(1))


def _ada_kernel(c_ref, w_ref, b_ref, o_ref):
    c = c_ref[...]
    a = (c * _sigmoid(c)).astype(BF16)
    o_ref[0] = _dot(a, w_ref[0].astype(BF16)) + b_ref[0]


def _ada_modulation(c_all, w_ada, b_ada):
    depth, d, n = w_ada.shape
    nb = c_all.shape[0]
    tn = 1536
    assert n % tn == 0
    return pl.pallas_call(
        _ada_kernel,
        grid=(depth, n // tn),
        in_specs=[pl.BlockSpec((nb, d), lambda l, j: (0, 0)),
                  pl.BlockSpec((1, d, tn), lambda l, j: (l, 0, j)),
                  pl.BlockSpec((1, 1, tn), lambda l, j: (l, 0, j))],
        out_specs=pl.BlockSpec((1, nb, tn), lambda l, j: (l, 0, j)),
        out_shape=jax.ShapeDtypeStruct((depth, nb, n), F32),
        compiler_params=_cparams("arbitrary", "arbitrary"),
        name="ada_modulation",
    )(c_all, w_ada, b_ada.reshape(depth, 1, n))


def _in_proj_kernel(x_ref, mod_ref, g_ref, w_ref, *out_refs):
    tb, tl, d = x_ref.shape
    x = x_ref[...]
    y = x * lax.rsqrt(jnp.mean(x * x, axis=-1, keepdims=True) + EPS) * g_ref[...]
    h = y * (1.0 + mod_ref[:, 1:2, :]) + mod_ref[:, 0:1, :]
    hb = h.reshape(tb * tl, d).astype(BF16)
    c0 = 0
    for o_ref in out_refs:
        n = o_ref.shape[-1]
        o_ref[...] = _dot(hb, w_ref[:, c0:c0 + n]).reshape(tb, tl, n)
        c0 += n


def _in_proj(x, mod, norm_g, w_in_all, layer):
    b, l, d = x.shape
    tb, tl = _row_tiles(b, l)
    n_all = w_in_all.shape[-1]
    return pl.pallas_call(
        _in_proj_kernel,
        grid=(b // tb, l // tl),
        in_specs=[pl.BlockSpec((tb, tl, d), lambda i, t: (i, t, 0)),
                  pl.BlockSpec((tb, N_MOD, d), lambda i, t: (i, 0, 0)),
                  _const_spec((1, d)),
                  _const_spec((d, n_all), layer)],
        out_specs=[pl.BlockSpec((tb, tl, n), lambda i, t: (i, t, 0)) for n in IN_WIDTHS],
        out_shape=[jax.ShapeDtypeStruct((b, l, n), F32) for n in IN_WIDTHS],
        compiler_params=_cparams("arbitrary", "arbitrary"),
        name="in_proj",
    )(x, mod, norm_g, w_in_all)


def _conv_kernel(u_ref, prev_ref, w_ref, b_ref, g_ref, beta_ref, y_ref, tail_ref, full_scr):
    tl = u_ref.shape[1]

    @pl.when(pl.program_id(1) == 0)
    def _():
        full_scr[0:CONV_PAD, :] = prev_ref[0]

    u = u_ref[0]
    full_scr[CONV_PAD:CONV_PAD + tl, :] = u[:, :GROUP_W] * _sigmoid(u[:, GROUP_W:])
    sub = min(tl, 64)
    first = CONV_PAD - (CONV_W - 1)
    for r in range(0, tl, sub):
        acc = jnp.zeros((sub, GROUP_W), F32)
        for k in range(CONV_W):
            acc = acc + w_ref[k:k + 1, :] * full_scr[first + r + k:first + r + k + sub, :]
        y = acc + b_ref[...]
        mu = jnp.mean(y, axis=-1, keepdims=True)
        yc = y - mu
        var = jnp.mean(yc * yc, axis=-1, keepdims=True)
        yn = yc * lax.rsqrt(var + EPS) * g_ref[...] + beta_ref[...]
        y_ref[0, r:r + sub, :] = (yn * _sigmoid(yn)).astype(y_ref.dtype)
    tail = full_scr[tl:tl + CONV_PAD, :]
    tail_ref[0] = tail
    full_scr[0:CONV_PAD, :] = tail


def _conv_mixer(u_conv, prev, w, bias, ln_g, ln_b):
    b, l, _ = u_conv.shape
    tl = min(l, ROW_TILE)
    prev_p = jnp.pad(prev, ((0, 0), (CONV_PAD - (CONV_W - 1), 0), (0, 0)))
    y, tail = pl.pallas_call(
        _conv_kernel,
        grid=(b, l // tl),
        in_specs=[pl.BlockSpec((1, tl, 2 * GROUP_W), lambda i, t: (i, t, 0)),
                  pl.BlockSpec((1, CONV_PAD, GROUP_W), lambda i, t: (i, 0, 0)),
                  _const_spec((CONV_W, GROUP_W)),
                  _const_spec((1, GROUP_W)), _const_spec((1, GROUP_W)), _const_spec((1, GROUP_W))],
        out_specs=[pl.BlockSpec((1, tl, GROUP_W), lambda i, t: (i, t, 0)),
                   pl.BlockSpec((1, CONV_PAD, GROUP_W), lambda i, t: (i, 0, 0))],
        out_shape=[jax.ShapeDtypeStruct((b, l, GROUP_W), BF16),
                   jax.ShapeDtypeStruct((b, CONV_PAD, GROUP_W), F32)],
        scratch_shapes=[pltpu.VMEM((CONV_PAD + tl, GROUP_W), F32)],
        compiler_params=_cparams("arbitrary", "arbitrary"),
        name="conv_mixer",
    )(u_conv, prev_p, w, bias, ln_g, ln_b)
    return y, tail[:, CONV_PAD - (CONV_W - 1):]


def _pool_kernel(u_ref, prev_ref, w_ref, scale_ref, y_ref, tail_ref, full_scr, *, pos0):
    tl = u_ref.shape[1]
    t = pl.program_id(1)

    @pl.when(t == 0)
    def _():
        full_scr[0:POOL_PAD, :] = prev_ref[0]

    cur = u_ref[0]
    full_scr[POOL_PAD:POOL_PAD + tl, :] = cur
    lane = lax.broadcasted_iota(jnp.int32, (tl, GROUP_W), 1)
    row = lax.broadcasted_iota(jnp.int32, (tl, GROUP_W), 0)
    group = lane // (GROUP_W // 4)
    acc = cur
    win = None
    for i in range(1, 16):
        acc = acc + full_scr[POOL_PAD - i:POOL_PAD - i + tl, :]
        if i in (1, 3, 7, 15):
            gi = (1, 3, 7, 15).index(i)
            win = acc if win is None else jnp.where(group >= gi, acc, win)
    width = jnp.left_shift(2, group)
    cnt = jnp.minimum(pos0 + t * tl + row + 1, width).astype(F32)
    pooled = win / cnt - cur
    y = _dot(pooled.astype(BF16), w_ref[...]) * scale_ref[...]
    y_ref[0] = y.astype(y_ref.dtype)
    tail = full_scr[tl:tl + POOL_PAD, :]
    tail_ref[0] = tail
    full_scr[0:POOL_PAD, :] = tail


def _pool_mixer(u_pool, prev, w_bd, scale, pos0):
    b, l, _ = u_pool.shape
    tl = min(l, ROW_TILE)
    prev_p = jnp.pad(prev, ((0, 0), (POOL_PAD - POOL_PREFIX, 0), (0, 0)))
    y, tail = pl.pallas_call(
        functools.partial(_pool_kernel, pos0=pos0),
        grid=(b, l // tl),
        in_specs=[pl.BlockSpec((1, tl, GROUP_W), lambda i, t: (i, t, 0)),
                  pl.BlockSpec((1, POOL_PAD, GROUP_W), lambda i, t: (i, 0, 0)),
                  _const_spec((GROUP_W, GROUP_W)), _const_spec((1, GROUP_W))],
        out_specs=[pl.BlockSpec((1, tl, GROUP_W), lambda i, t: (i, t, 0)),
                   pl.BlockSpec((1, POOL_PAD, GROUP_W), lambda i, t: (i, 0, 0))],
        out_shape=[jax.ShapeDtypeStruct((b, l, GROUP_W), BF16),
                   jax.ShapeDtypeStruct((b, POOL_PAD, GROUP_W), F32)],
        scratch_shapes=[pltpu.VMEM((POOL_PAD + tl, GROUP_W), F32)],
        compiler_params=_cparams("arbitrary", "arbitrary"),
        name="pool_mixer",
    )(u_pool, prev_p, w_bd, scale)
    return y, tail[:, POOL_PAD - POOL_PREFIX:]


def _head_ones(n):
    r = lax.broadcasted_iota(jnp.int32, (n, n), 0) // DN_HEAD_DIM
    c = lax.broadcasted_iota(jnp.int32, (n, n), 1) // DN_HEAD_DIM
    return (r == c).astype(F32)


def _unit_lower_inverse(lmat):
    c = lmat.shape[0]
    eye = (lax.broadcasted_iota(jnp.int32, (c, c), 0) == lax.broadcasted_iota(jnp.int32, (c, c), 1)).astype(F32)
    x = eye - lmat
    p = _dot(lmat, lmat, HI)
    span = 2
    while True:
        x = x + _dot(x, p, HI)
        span *= 2
        if span >= c:
            return x
        p = _dot(p, p, HI)


def _dn_kernel(qkv_ref, gate_ref, ab_ref, prev_ref, s0_ref, cw_ref, alog_ref, dtb_ref, ng_ref,
               y_ref, s_ref, tail_ref, full_scr, q_scr, k_scr, v_scr, g_scr, b_scr, o_scr):
    tl = qkv_ref.shape[1]
    gw = GROUP_W

    @pl.when(pl.program_id(1) == 0)
    def _():
        full_scr[0:DN_PAD, :] = prev_ref[0]
        s_ref[0] = s0_ref[0]

    full_scr[DN_PAD:DN_PAD + tl, :] = qkv_ref[0]
    first = DN_PAD - (DN_CONV - 1)
    acc = cw_ref[0:1, :] * full_scr[first:first + tl, :]
    for k in range(1, DN_CONV):
        acc = acc + cw_ref[k:k + 1, :] * full_scr[first + k:first + k + tl, :]
    qkv = acc * _sigmoid(acc)
    tail = full_scr[tl:tl + DN_PAD, :]
    tail_ref[0] = tail
    full_scr[0:DN_PAD, :] = tail

    ones_bd = _head_ones(gw)
    q = qkv[:, 0:gw]
    k = qkv[:, gw:2 * gw]
    q_scr[...] = q * lax.rsqrt(_dot(q * q, ones_bd, HI) + EPS) * (DN_HEAD_DIM ** -0.5)
    k_scr[...] = k * lax.rsqrt(_dot(k * k, ones_bd, HI) + EPS)
    v_scr[...] = qkv[:, 2 * gw:3 * gw]

    ab = ab_ref[0]
    g_all = -jnp.exp(alog_ref[...]) * _softplus(ab + dtb_ref[...])
    lane = lax.broadcasted_iota(jnp.int32, ab.shape, 1)
    gb = jnp.where(lane < DN_HEADS, g_all, _sigmoid(ab))
    er = lax.broadcasted_iota(jnp.int32, (AB_W, 2 * gw), 0)
    ec = lax.broadcasted_iota(jnp.int32, (AB_W, 2 * gw), 1)
    expand = (er == jnp.where(ec < gw, ec // DN_HEAD_DIM, DN_HEADS + (ec - gw) // DN_HEAD_DIM)).astype(F32)
    gbx = _dot(gb, expand, HI)
    g_scr[...] = gbx[:, 0:gw]
    b_scr[...] = gbx[:, gw:2 * gw]

    ci = lax.broadcasted_iota(jnp.int32, (CHUNK, CHUNK), 0)
    cj = lax.broadcasted_iota(jnp.int32, (CHUNK, CHUNK), 1)
    causal = ci >= cj
    strict = ci > cj
    tri = causal.astype(F32)

    def chunk(c, carry):
        rows = pl.ds(pl.multiple_of(c * CHUNK, CHUNK), CHUNK)
        qc, kc, vc, bc = q_scr[rows, :], k_scr[rows, :], v_scr[rows, :], b_scr[rows, :]
        gcb = _dot(tri, g_scr[rows, :], HI)
        eg = jnp.exp(gcb)
        glast = gcb[CHUNK - 1:CHUNK, :]
        kb = kc * bc
        qd = qc * eg
        ku = kc * jnp.exp(glast - gcb)
        wr = kb * eg
        vb = vc * bc
        outs = []
        for h in range(DN_HEADS):
            hs = slice(h * DN_HEAD_DIM, (h + 1) * DN_HEAD_DIM)
            gh = gcb[:, hs]
            diff = gh - gh.T
            decay = jnp.where(causal, jnp.exp(jnp.where(causal, diff, 0.0)), 0.0)
            kh = kc[:, hs].astype(BF16)
            lmat = jnp.where(strict, _dot_nt(kb[:, hs].astype(BF16), kh) * decay, 0.0)
            tinv = _unit_lower_inverse(lmat)
            sol = _dot(tinv, jnp.concatenate([vb[:, hs], wr[:, hs]], axis=1), HI)
            u_c = sol[:, :DN_HEAD_DIM]
            w_c = sol[:, DN_HEAD_DIM:]
            attn = _dot_nt(qc[:, hs].astype(BF16), kh) * decay
            s_old = s_ref[0, h]
            s_bf = s_old.astype(BF16)
            v_new = u_c - _dot(w_c.astype(BF16), s_bf)
            vn_bf = v_new.astype(BF16)
            outs.append(_dot(qd[:, hs].astype(BF16), s_bf) + _dot(attn.astype(BF16), vn_bf))
            s_ref[0, h] = s_old * jnp.exp(glast[:, hs]) + _dot_tn(ku[:, hs].astype(BF16), vn_bf)
        o_scr[rows, :] = jnp.concatenate(outs, axis=1)
        return carry

    lax.fori_loop(0, tl // CHUNK, chunk, 0)

    o = o_scr[...]
    ms = _dot(o * o, ones_bd, HI) * (1.0 / DN_HEAD_DIM)
    gate = gate_ref[0]
    y_ref[0] = (o * lax.rsqrt(ms + EPS) * ng_ref[...] * (gate * _sigmoid(gate))).astype(y_ref.dtype)


def _dn_mixer(u_qkv, u_gate, u_ab, prev, s0, conv_w, a_log, dt_bias, norm_g):
    b, l, _ = u_qkv.shape
    assert l % CHUNK == 0
    tl = min(l, ROW_TILE)
    gw3 = 3 * GROUP_W
    prev_p = jnp.pad(prev, ((0, 0), (DN_PAD - (DN_CONV - 1), 0), (0, 0)))
    alog_p = jnp.pad(a_log, (0, AB_W - DN_HEADS)).reshape(1, AB_W)
    dtb_p = jnp.pad(dt_bias, (0, AB_W - DN_HEADS)).reshape(1, AB_W)
    ng_p = jnp.tile(norm_g, DN_HEADS).reshape(1, GROUP_W)
    state_spec = pl.BlockSpec((1, DN_HEADS, DN_HEAD_DIM, DN_HEAD_DIM), lambda i, t: (i, 0, 0, 0))
    y, s_new, tail = pl.pallas_call(
        _dn_kernel,
        grid=(b, l // tl),
        in_specs=[pl.BlockSpec((1, tl, gw3), lambda i, t: (i, t, 0)),
                  pl.BlockSpec((1, tl, GROUP_W), lambda i, t: (i, t, 0)),
                  pl.BlockSpec((1, tl, AB_W), lambda i, t: (i, t, 0)),
                  pl.BlockSpec((1, DN_PAD, gw3), lambda i, t: (i, 0, 0)),
                  state_spec,
                  _const_spec((DN_CONV, gw3)), _const_spec((1, AB_W)), _const_spec((1, AB_W)),
                  _const_spec((1, GROUP_W))],
        out_specs=[pl.BlockSpec((1, tl, GROUP_W), lambda i, t: (i, t, 0)),
                   state_spec,
                   pl.BlockSpec((1, DN_PAD, gw3), lambda i, t: (i, 0, 0))],
        out_shape=[jax.ShapeDtypeStruct((b, l, GROUP_W), BF16),
                   jax.ShapeDtypeStruct((b, DN_HEADS, DN_HEAD_DIM, DN_HEAD_DIM), F32),
                   jax.ShapeDtypeStruct((b, DN_PAD, gw3), F32)],
        scratch_shapes=[pltpu.VMEM((DN_PAD + tl, gw3), F32)] + [pltpu.VMEM((tl, GROUP_W), F32)] * 6,
        compiler_params=_cparams("arbitrary", "arbitrary"),
        name="deltanet_mixer",
    )(u_qkv, u_gate, u_ab, prev_p, s0, conv_w, alog_p, dtb_p, ng_p)
    return y, s_new, tail[:, DN_PAD - (DN_CONV - 1):]


def _sb_prep_kernel(u_ref, qg_ref, kg_ref, q_ref, k_ref, v_ref, kb_ref, vb_ref):
    u = u_ref[0]
    for h in range(SB_HEADS):
        lo = h * SB_HEAD_DIM
        q = u[:, lo:lo + SB_HEAD_DIM]
        k = u[:, GROUP_W + lo:GROUP_W + lo + SB_HEAD_DIM]
        v = u[:, 2 * GROUP_W + lo:2 * GROUP_W + lo + SB_HEAD_DIM]
        qn = q * lax.rsqrt(jnp.mean(q * q, axis=-1, keepdims=True) + EPS) * qg_ref[...]
        kn = k * lax.rsqrt(jnp.mean(k * k, axis=-1, keepdims=True) + EPS) * kg_ref[...]
        q_ref[0, h] = (qn * (SB_HEAD_DIM ** -0.5)).astype(BF16)
        k_ref[0, h] = kn
        v_ref[0, h] = v
        kb_ref[0, h] = kn.astype(BF16)
        vb_ref[0, h] = v.astype(BF16)


def _sb_prep(u_sb, q_norm, k_norm):
    b, l, _ = u_sb.shape
    tl = min(l, ROW_TILE)
    head_spec = pl.BlockSpec((1, SB_HEADS, tl, SB_HEAD_DIM), lambda i, t: (i, 0, t, 0))
    shape = (b, SB_HEADS, l, SB_HEAD_DIM)
    return pl.pallas_call(
        _sb_prep_kernel,
        grid=(b, l // tl),
        in_specs=[pl.BlockSpec((1, tl, 3 * GROUP_W), lambda i, t: (i, t, 0)),
                  _const_spec((1, SB_HEAD_DIM)), _const_spec((1, SB_HEAD_DIM))],
        out_specs=[head_spec] * 5,
        out_shape=[jax.ShapeDtypeStruct(shape, BF16), jax.ShapeDtypeStruct(shape, F32),
                   jax.ShapeDtypeStruct(shape, F32), jax.ShapeDtypeStruct(shape, BF16),
                   jax.ShapeDtypeStruct(shape, BF16)],
        compiler_params=_cparams("arbitrary", "arbitrary"),
        name="sb_prep",
    )(u_sb, q_norm, k_norm)


def _sb_kernel(q_ref, k_ref, v_ref, o_ref, *, q_off, qb, kb):
    q0 = q_off + pl.program_id(2) * qb
    nk = (q0 + qb - 1) // kb + 1
    q = q_ref[0, 0]
    r = lax.broadcasted_iota(jnp.int32, (kb, 2 * kb), 0)
    c = lax.broadcasted_iota(jnp.int32, (kb, 2 * kb), 1)
    suffix = ((r >= c) | (c >= kb)).astype(BF16)
    qpos = q0 + lax.broadcasted_iota(jnp.int32, (qb, kb), 0)
    kcol = lax.broadcasted_iota(jnp.int32, (qb, kb), 1)

    def tile(j, later, acc, masked):
        ks = pl.ds(pl.multiple_of(j * kb, kb), kb)
        z = _dot_nt(q, k_ref[0, 0, ks, :])
        if masked:
            z = jnp.where(j * kb + kcol < qpos, z, SB_NEG)
        sp = _softplus(z)
        cs = _dot(sp.astype(BF16), suffix)
        a = jnp.exp(z - cs[:, :kb] - later)
        acc = acc + _dot(a.astype(BF16), v_ref[0, 0, ks, :])
        return later + cs[:, kb:], acc

    later = jnp.zeros((qb, kb), F32)
    acc = jnp.zeros((qb, SB_HEAD_DIM), F32)
    later, acc = tile(nk - 1, later, acc, True)

    def body(i, carry):
        return tile(nk - 2 - i, carry[0], carry[1], False)

    later, acc = lax.fori_loop(0, nk - 1, body, (later, acc))
    o_ref[0, 0] = acc.astype(o_ref.dtype)


def _sb_attention(q, k_all, v_all, q_off):
    b, h, lq, d = q.shape
    lk = k_all.shape[2]
    qb = min(SB_BLOCK, lq)
    kb = SB_BLOCK
    assert lq % qb == 0 and lk % kb == 0 and q_off % kb == 0 and kb % qb == 0
    assert lk >= q_off + lq
    return pl.pallas_call(
        functools.partial(_sb_kernel, q_off=q_off, qb=qb, kb=kb),
        grid=(b, h, lq // qb),
        in_specs=[pl.BlockSpec((1, 1, qb, d), lambda i, j, t: (i, j, t, 0)),
                  pl.BlockSpec((1, 1, lk, d), lambda i, j, t: (i, j, 0, 0)),
                  pl.BlockSpec((1, 1, lk, d), lambda i, j, t: (i, j, 0, 0))],
        out_specs=pl.BlockSpec((1, 1, qb, d), lambda i, j, t: (i, j, t, 0)),
        out_shape=jax.ShapeDtypeStruct((b, h, lq, d), BF16),
        compiler_params=_cparams("arbitrary", "arbitrary", "arbitrary"),
        name="sb_attention",
    )(q, k_all, v_all)


def _sb_mixer(u_sb, k_prev, v_prev, q_norm, k_norm):
    q, k_new, v_new, k_bf, v_bf = _sb_prep(u_sb, q_norm, k_norm)
    past = k_prev.shape[2]
    lq = u_sb.shape[1]
    pad = (-(past + lq)) % SB_BLOCK
    if past or pad:
        widths = ((0, 0), (0, 0), (0, pad), (0, 0))
        k_bf = jnp.pad(jnp.concatenate([k_prev.astype(BF16), k_bf], axis=2), widths)
        v_bf = jnp.pad(jnp.concatenate([v_prev.astype(BF16), v_bf], axis=2), widths)
    return _sb_attention(q, k_bf, v_bf, past), k_new, v_new


def _out_ffn_kernel(x_ref, yc_ref, yd_ref, yp_ref, ys_ref, mod_ref, g_ref, wo_ref, wg_ref, wu_ref, wd_ref, o_ref):
    tb, tl, d = x_ref.shape
    m = tb * tl
    gw = GROUP_W
    mix = _dot(yc_ref[...].reshape(m, gw), wo_ref[0:gw, :])
    mix = mix + _dot(yd_ref[...].reshape(m, gw), wo_ref[gw:2 * gw, :])
    mix = mix + _dot(yp_ref[...].reshape(m, gw), wo_ref[2 * gw:3 * gw, :])
    for h in range(SB_HEADS):
        lo = 3 * gw + h * SB_HEAD_DIM
        mix = mix + _dot(ys_ref[:, h].reshape(m, SB_HEAD_DIM), wo_ref[lo:lo + SB_HEAD_DIM, :])
    x1 = x_ref[...] + mod_ref[:, 2:3, :] * mix.reshape(tb, tl, d)
    y = x1 * lax.rsqrt(jnp.mean(x1 * x1, axis=-1, keepdims=True) + EPS) * g_ref[...]
    h2 = (y * (1.0 + mod_ref[:, 4:5, :]) + mod_ref[:, 3:4, :]).reshape(m, d).astype(BF16)
    ff = D_FF // FF_SPLIT
    acc = jnp.zeros((m, d), F32)
    for c in range(FF_SPLIT):
        gate = _dot(h2, wg_ref[:, c * ff:(c + 1) * ff])
        up = _dot(h2, wu_ref[:, c * ff:(c + 1) * ff])
        act = (gate * _sigmoid(gate) * up).astype(BF16)
        acc = acc + _dot(act, wd_ref[c * ff:(c + 1) * ff, :])
    o_ref[...] = x1 + mod_ref[:, 5:6, :] * acc.reshape(tb, tl, d)


def _out_ffn(x, y_conv, y_dn, y_pool, y_sb, mod, norm_g, w_out, w_gate, w_up, w_down, layer):
    b, l, d = x.shape
    tb, tl = _row_tiles(b, l)
    assert (D_FF // FF_SPLIT) % V7X_LANES == 0
    row_spec = lambda n: pl.BlockSpec((tb, tl, n), lambda i, t: (i, t, 0))
    return pl.pallas_call(
        _out_ffn_kernel,
        grid=(b // tb, l // tl),
        in_specs=[row_spec(d), row_spec(GROUP_W), row_spec(GROUP_W), row_spec(GROUP_W),
                  pl.BlockSpec((tb, SB_HEADS, tl, SB_HEAD_DIM), lambda i, t: (i, 0, t, 0)),
                  pl.BlockSpec((tb, N_MOD, d), lambda i, t: (i, 0, 0)),
                  _const_spec((1, d)),
                  _const_spec((d, d), layer), _const_spec((d, D_FF), layer),
                  _const_spec((d, D_FF), layer), _const_spec((D_FF, d), layer)],
        out_specs=row_spec(d),
        out_shape=jax.ShapeDtypeStruct((b, l, d), F32),
        compiler_params=_cparams("arbitrary", "arbitrary"),
        name="out_ffn",
    )(x, y_conv, y_dn, y_pool, y_sb, mod, norm_g, w_out, w_gate, w_up, w_down)


def _pack_w_in(w_in):
    c0 = 2 * GROUP_W + 3 * GROUP_W + GROUP_W
    ab = w_in[..., c0:c0 + 2 * DN_HEADS]
    ab = jnp.pad(ab, ((0, 0), (0, 0), (0, AB_W - 2 * DN_HEADS)))
    return jnp.concatenate([w_in[..., :c0], ab, w_in[..., c0 + 2 * DN_HEADS:]], axis=-1).astype(BF16)


def _pool_block_diag(pool_w):
    g, n, _ = pool_w.shape
    out = jnp.zeros((g * n, g * n), pool_w.dtype)
    for i in range(g):
        out = out.at[i * n:(i + 1) * n, i * n:(i + 1) * n].set(pool_w[i])
    return out.astype(BF16)


def _trunk_layer(x, mod, pos0, conv_prev, dn_s0, dn_conv_prev, pool_prev, k_prev, v_prev, p, layer):
    u_conv, u_qkv, u_gate, u_ab, u_pool, u_sb = _in_proj(x, mod, p["norm_mix"], p["w_in"], layer)
    y_conv, conv_new = _conv_mixer(u_conv, conv_prev, p["conv_dw_w"], p["conv_dw_b"], p["conv_ln_g"], p["conv_ln_b"])
    y_dn, s_new, dn_conv_new = _dn_mixer(u_qkv, u_gate, u_ab, dn_conv_prev, dn_s0, p["dn_conv_w"],
                                         p["dn_a_log"], p["dn_dt_bias"], p["dn_norm_g"])
    y_pool, pool_new = _pool_mixer(u_pool, pool_prev, p["pool_w"], p["pool_scale"], pos0)
    y_sb, k_new, v_new = _sb_mixer(u_sb, k_prev, v_prev, p["sb_q_norm"], p["sb_k_norm"])
    x = _out_ffn(x, y_conv, y_dn, y_pool, y_sb, mod, p["norm_ffn"], p["w_out"], p["ffn_w_gate"],
                 p["ffn_w_up"], p["ffn_w_down"], layer)
    return x, conv_new, s_new, dn_conv_new, pool_new, k_new, v_new


def kernel(x_prompt, x_sample, c_prompt, c_sample, cache_conv, state_dn, cache_dn_conv, cache_pool, cache_sb_k, cache_sb_v, w_ada, b_ada, norm_mix, norm_ffn, w_in, w_out, conv_dw_w, conv_dw_b, conv_ln_g, conv_ln_b, dn_conv_w, dn_a_log, dn_dt_bias, dn_norm_g, pool_w, pool_scale, sb_q_norm, sb_k_norm, ffn_w_gate, ffn_w_up, ffn_w_down):
    depth = w_ada.shape[0]
    bp = x_prompt.shape[0]
    bs = x_sample.shape[0]
    past = cache_sb_k.shape[3]
    dt = x_prompt.dtype

    mod_all = _ada_modulation(jnp.concatenate([c_prompt, c_sample], axis=0), w_ada, b_ada)
    mod_all = mod_all.reshape(depth, bp + bs, N_MOD, D_MODEL)
    w_in_p = _pack_w_in(w_in)
    w_out_b, w_gate_b, w_up_b, w_down_b = (w.astype(BF16) for w in (w_out, ffn_w_gate, ffn_w_up, ffn_w_down))

    zeros_p = dict(
        conv=jnp.zeros((bp, CONV_W - 1, GROUP_W), dt),
        s0=jnp.zeros((bp, DN_HEADS, DN_HEAD_DIM, DN_HEAD_DIM), F32),
        dn_conv=jnp.zeros((bp, DN_CONV - 1, 3 * GROUP_W), dt),
        pool=jnp.zeros((bp, POOL_PREFIX, GROUP_W), dt),
        kv=jnp.zeros((bp, SB_HEADS, 0, SB_HEAD_DIM), dt))

    xp, xs = x_prompt, x_sample
    new_p = [[] for _ in range(6)]
    new_s = [[] for _ in range(6)]
    for l in range(depth):
        row = lambda a: a[l].reshape(1, -1)
        p = dict(norm_mix=row(norm_mix), norm_ffn=row(norm_ffn), w_in=w_in_p, w_out=w_out_b,
                 conv_dw_w=conv_dw_w[l], conv_dw_b=row(conv_dw_b), conv_ln_g=row(conv_ln_g), conv_ln_b=row(conv_ln_b),
                 dn_conv_w=dn_conv_w[l], dn_a_log=dn_a_log[l], dn_dt_bias=dn_dt_bias[l], dn_norm_g=dn_norm_g[l],
                 pool_w=_pool_block_diag(pool_w[l]), pool_scale=row(pool_scale),
                 sb_q_norm=row(sb_q_norm), sb_k_norm=row(sb_k_norm),
                 ffn_w_gate=w_gate_b, ffn_w_up=w_up_b, ffn_w_down=w_down_b)
        xp, *sp = _trunk_layer(xp, mod_all[l, :bp], 0, zeros_p["conv"], zeros_p["s0"], zeros_p["dn_conv"],
                               zeros_p["pool"], zeros_p["kv"], zeros_p["kv"], p, l)
        xs, *ss = _trunk_layer(xs, mod_all[l, bp:], past, cache_conv[l], state_dn[l], cache_dn_conv[l],
                               cache_pool[l], cache_sb_k[l], cache_sb_v[l], p, l)
        for i in range(6):
            new_p[i].append(sp[i])
            new_s[i].append(ss[i])
    conv_p, dn_p, dnconv_p, pool_p, sbk_p, sbv_p = [jnp.stack(a) for a in new_p]
    conv_s, dn_s, dnconv_s, pool_s, sbk_s, sbv_s = [jnp.stack(a) for a in new_s]
    return (xp, xs, conv_p, conv_s, dn_p, dn_s, dnconv_p, dnconv_s, pool_p, pool_s, sbk_p, sbk_s, sbv_p, sbv_s)
```

```python
import functools

import jax
import jax.numpy as jnp
from jax import lax
from jax.experimental import pallas as pl
from jax.experimental.pallas import tpu as pltpu

D_MODEL = 1024
GROUP_W = 256
CONV_W = 31
DN_HEADS = 4
DN_HEAD_DIM = 64
DN_CONV = 4
CHUNK = 64
POOL_PREFIX = 15
SB_HEADS = 4
SB_HEAD_DIM = 64
SB_NEG = -1e30
D_FF = 2816
N_MOD = 6
EPS = 1e-6

V7X_LANES = 128
V7X_SUBLANES = 8
V7X_MXU_DIM = 256
V7X_VMEM_BYTES = 64 * 1024 * 1024
VMEM_LIMIT = V7X_VMEM_BYTES * 7 // 8

ROW_TILE = 512
FF_SPLIT = 2
CONV_PAD = 32
DN_PAD = 8
POOL_PAD = 16
AB_W = V7X_LANES
DN_UNROLL = 2
SB_BLOCK = V7X_MXU_DIM
IN_WIDTHS = (2 * GROUP_W, 3 * GROUP_W, GROUP_W, AB_W, GROUP_W, 3 * GROUP_W)

F32 = jnp.float32
BF16 = jnp.bfloat16


def _cparams(*sem):
    return pltpu.CompilerParams(dimension_semantics=sem, vmem_limit_bytes=VMEM_LIMIT)


def _dot(a, b):
    return jnp.dot(a, b, preferred_element_type=F32)


def _dot_nt(a, b):
    return lax.dot_general(a, b, (((1,), (1,)), ((), ())), preferred_element_type=F32)


def _dot_tn(a, b):
    return lax.dot_general(a, b, (((0,), (0,)), ((), ())), preferred_element_type=F32)


def _split(x, parts):
    out = []
    for _ in range(parts):
        p = x.astype(BF16)
        out.append(p)
        x = x - p.astype(F32)
    return out


def _dot3(a, b):
    ah, al = _split(a, 2)
    bh, bl = _split(b, 2)
    return _dot(ah, bh) + (_dot(al, bh) + _dot(ah, bl))


def _dot_sel_rhs(a, sel, parts):
    pieces = _split(a, parts)
    out = _dot(pieces[0], sel)
    for p in pieces[1:]:
        out = out + _dot(p, sel)
    return out


def _dot_sel_lhs(sel, b, parts):
    pieces = _split(b, parts)
    out = _dot(sel, pieces[0])
    for p in pieces[1:]:
        out = out + _dot(sel, p)
    return out


def _sigmoid(x):
    return 1.0 / (1.0 + jnp.exp(-x))


def _softplus(x):
    return jnp.maximum(x, 0.0) + jnp.log1p(jnp.exp(-jnp.abs(x)))


def _row_tiles(batch, length):
    if length >= ROW_TILE:
        assert length % ROW_TILE == 0
        return 1, ROW_TILE
    tb = max(1, min(batch, ROW_TILE // length))
    while batch % tb:
        tb -= 1
    return tb, length


def _const_spec(shape, layer=None):
    nd = len(shape)
    if layer is None:
        return pl.BlockSpec(shape, lambda *_: (0,) * nd, pipeline_mode=pl.Buffered(1))
    return pl.BlockSpec((None,) + shape, lambda *_: (layer,) + (0,) * nd, pipeline_mode=pl.Buffered(1))


def _head_ones(n, dtype):
    r = lax.broadcasted_iota(jnp.int32, (n, n), 0) // DN_HEAD_DIM
    c = lax.broadcasted_iota(jnp.int32, (n, n), 1) // DN_HEAD_DIM
    return (r == c).astype(dtype)


def _ada_kernel(c_ref, w_ref, b_ref, o_ref):
    c = c_ref[...]
    a = (c * _sigmoid(c)).astype(BF16)
    o_ref[0] = _dot(a, w_ref[0].astype(BF16)) + b_ref[0]


def _ada_modulation(c_all, w_ada, b_ada):
    depth, d, n = w_ada.shape
    nb = c_all.shape[0]
    tn = 1536
    assert n % tn == 0
    return pl.pallas_call(
        _ada_kernel,
        grid=(depth, n // tn),
        in_specs=[pl.BlockSpec((nb, d), lambda l, j: (0, 0)),
                  pl.BlockSpec((1, d, tn), lambda l, j: (l, 0, j)),
                  pl.BlockSpec((1, 1, tn), lambda l, j: (l, 0, j))],
        out_specs=pl.BlockSpec((1, nb, tn), lambda l, j: (l, 0, j)),
        out_shape=jax.ShapeDtypeStruct((depth, nb, n), F32),
        compiler_params=_cparams("arbitrary", "arbitrary"),
        name="ada_modulation",
    )(c_all, w_ada, b_ada.reshape(depth, 1, n))


def _in_proj_kernel(x_ref, mod_ref, g_ref, w_ref, *out_refs):
    tb, tl, d = x_ref.shape
    x = x_ref[...]
    y = x * lax.rsqrt(jnp.mean(x * x, axis=-1, keepdims=True) + EPS) * g_ref[...]
    h = y * (1.0 + mod_ref[:, 1:2, :]) + mod_ref[:, 0:1, :]
    hb = h.reshape(tb * tl, d).astype(BF16)
    c0 = 0
    for o_ref in out_refs:
        n = o_ref.shape[-1]
        o_ref[...] = _dot(hb, w_ref[:, c0:c0 + n]).reshape(tb, tl, n)
        c0 += n


def _in_proj(x, mod, norm_g, w_in_all, layer):
    b, l, d = x.shape
    tb, tl = _row_tiles(b, l)
    n_all = w_in_all.shape[-1]
    return pl.pallas_call(
        _in_proj_kernel,
        grid=(b // tb, l // tl),
        in_specs=[pl.BlockSpec((tb, tl, d), lambda i, t: (i, t, 0)),
                  pl.BlockSpec((tb, N_MOD, d), lambda i, t: (i, 0, 0)),
                  _const_spec((1, d)),
                  _const_spec((d, n_all), layer)],
        out_specs=[pl.BlockSpec((tb, tl, n), lambda i, t: (i, t, 0)) for n in IN_WIDTHS],
        out_shape=[jax.ShapeDtypeStruct((b, l, n), F32) for n in IN_WIDTHS],
        compiler_params=_cparams("arbitrary", "arbitrary"),
        name="in_proj",
    )(x, mod, norm_g, w_in_all)


def _conv_kernel(u_ref, prev_ref, w_ref, b_ref, g_ref, beta_ref, y_ref, tail_ref, full_scr):
    tl = u_ref.shape[1]

    @pl.when(pl.program_id(1) == 0)
    def _():
        full_scr[0:CONV_PAD, :] = prev_ref[0]

    u = u_ref[0]
    full_scr[CONV_PAD:CONV_PAD + tl, :] = u[:, :GROUP_W] * _sigmoid(u[:, GROUP_W:])
    sub = min(tl, 64)
    first = CONV_PAD - (CONV_W - 1)
    for r in range(0, tl, sub):
        acc = jnp.zeros((sub, GROUP_W), F32)
        for k in range(CONV_W):
            acc = acc + w_ref[k:k + 1, :] * full_scr[first + r + k:first + r + k + sub, :]
        y = acc + b_ref[...]
        mu = jnp.mean(y, axis=-1, keepdims=True)
        yc = y - mu
        var = jnp.mean(yc * yc, axis=-1, keepdims=True)
        yn = yc * lax.rsqrt(var + EPS) * g_ref[...] + beta_ref[...]
        y_ref[0, r:r + sub, :] = (yn * _sigmoid(yn)).astype(y_ref.dtype)
    tail = full_scr[tl:tl + CONV_PAD, :]
    tail_ref[0] = tail
    full_scr[0:CONV_PAD, :] = tail


def _conv_mixer(u_conv, prev, w, bias, ln_g, ln_b):
    b, l, _ = u_conv.shape
    tl = min(l, ROW_TILE)
    prev_p = jnp.pad(prev, ((0, 0), (CONV_PAD - (CONV_W - 1), 0), (0, 0)))
    y, tail = pl.pallas_call(
        _conv_kernel,
        grid=(b, l // tl),
        in_specs=[pl.BlockSpec((1, tl, 2 * GROUP_W), lambda i, t: (i, t, 0)),
                  pl.BlockSpec((1, CONV_PAD, GROUP_W), lambda i, t: (i, 0, 0)),
                  _const_spec((CONV_W, GROUP_W)),
                  _const_spec((1, GROUP_W)), _const_spec((1, GROUP_W)), _const_spec((1, GROUP_W))],
        out_specs=[pl.BlockSpec((1, tl, GROUP_W), lambda i, t: (i, t, 0)),
                   pl.BlockSpec((1, CONV_PAD, GROUP_W), lambda i, t: (i, 0, 0))],
        out_shape=[jax.ShapeDtypeStruct((b, l, GROUP_W), BF16),
                   jax.ShapeDtypeStruct((b, CONV_PAD, GROUP_W), F32)],
        scratch_shapes=[pltpu.VMEM((CONV_PAD + tl, GROUP_W), F32)],
        compiler_params=_cparams("arbitrary", "arbitrary"),
        name="conv_mixer",
    )(u_conv, prev_p, w, bias, ln_g, ln_b)
    return y, tail[:, CONV_PAD - (CONV_W - 1):]


def _pool_kernel(u_ref, prev_ref, w_ref, scale_ref, y_ref, tail_ref, full_scr, *, pos0):
    tl = u_ref.shape[1]
    t = pl.program_id(1)

    @pl.when(t == 0)
    def _():
        full_scr[0:POOL_PAD, :] = prev_ref[0]

    cur = u_ref[0]
    full_scr[POOL_PAD:POOL_PAD + tl, :] = cur
    lane = lax.broadcasted_iota(jnp.int32, (tl, GROUP_W), 1)
    row = lax.broadcasted_iota(jnp.int32, (tl, GROUP_W), 0)
    group = lane // (GROUP_W // 4)
    acc = cur
    win = None
    for i in range(1, 16):
        acc = acc + full_scr[POOL_PAD - i:POOL_PAD - i + tl, :]
        if i in (1, 3, 7, 15):
            gi = (1, 3, 7, 15).index(i)
            win = acc if win is None else jnp.where(group >= gi, acc, win)
    width = jnp.left_shift(2, group)
    cnt = jnp.minimum(pos0 + t * tl + row + 1, width).astype(F32)
    pooled = win / cnt - cur
    y = _dot(pooled.astype(BF16), w_ref[...]) * scale_ref[...]
    y_ref[0] = y.astype(y_ref.dtype)
    tail = full_scr[tl:tl + POOL_PAD, :]
    tail_ref[0] = tail
    full_scr[0:POOL_PAD, :] = tail


def _pool_mixer(u_pool, prev, w_bd, scale, pos0):
    b, l, _ = u_pool.shape
    tl = min(l, ROW_TILE)
    prev_p = jnp.pad(prev, ((0, 0), (POOL_PAD - POOL_PREFIX, 0), (0, 0)))
    y, tail = pl.pallas_call(
        functools.partial(_pool_kernel, pos0=pos0),
        grid=(b, l // tl),
        in_specs=[pl.BlockSpec((1, tl, GROUP_W), lambda i, t: (i, t, 0)),
                  pl.BlockSpec((1, POOL_PAD, GROUP_W), lambda i, t: (i, 0, 0)),
                  _const_spec((GROUP_W, GROUP_W)), _const_spec((1, GROUP_W))],
        out_specs=[pl.BlockSpec((1, tl, GROUP_W), lambda i, t: (i, t, 0)),
                   pl.BlockSpec((1, POOL_PAD, GROUP_W), lambda i, t: (i, 0, 0))],
        out_shape=[jax.ShapeDtypeStruct((b, l, GROUP_W), BF16),
                   jax.ShapeDtypeStruct((b, POOL_PAD, GROUP_W), F32)],
        scratch_shapes=[pltpu.VMEM((POOL_PAD + tl, GROUP_W), F32)],
        compiler_params=_cparams("arbitrary", "arbitrary"),
        name="pool_mixer",
    )(u_pool, prev_p, w_bd, scale)
    return y, tail[:, POOL_PAD - POOL_PREFIX:]


def _unit_lower_inverses(lmats):
    c = lmats[0].shape[0]
    eye = (lax.broadcasted_iota(jnp.int32, (c, c), 0) == lax.broadcasted_iota(jnp.int32, (c, c), 1)).astype(F32)
    xs = [eye - m for m in lmats]
    ps = [_dot3(m, m) for m in lmats]
    span = 2
    while True:
        xs = [x + _dot3(x, p) for x, p in zip(xs, ps)]
        span *= 2
        if span >= c:
            return xs
        ps = [_dot3(p, p) for p in ps]


def _dn_kernel(qkv_ref, gate_ref, ab_ref, prev_ref, s0_ref, cw_ref, alog_ref, dtb_ref, ng_ref,
               y_ref, s_ref, tail_ref, full_scr, q_scr, k_scr, v_scr, g_scr, b_scr, o_scr):
    tl = qkv_ref.shape[1]
    gw = GROUP_W
    hd = DN_HEAD_DIM

    @pl.when(pl.program_id(1) == 0)
    def _():
        full_scr[0:DN_PAD, :] = prev_ref[0]
        s_ref[0] = s0_ref[0]

    full_scr[DN_PAD:DN_PAD + tl, :] = qkv_ref[0]
    first = DN_PAD - (DN_CONV - 1)
    acc = cw_ref[0:1, :] * full_scr[first:first + tl, :]
    for k in range(1, DN_CONV):
        acc = acc + cw_ref[k:k + 1, :] * full_scr[first + k:first + k + tl, :]
    qkv = acc * _sigmoid(acc)
    tail = full_scr[tl:tl + DN_PAD, :]
    tail_ref[0] = tail
    full_scr[0:DN_PAD, :] = tail

    ones_bd = _head_ones(gw, BF16)
    q = qkv[:, 0:gw]
    k = qkv[:, gw:2 * gw]
    q_scr[...] = q * lax.rsqrt(_dot_sel_rhs(q * q, ones_bd, 2) + EPS) * (hd ** -0.5)
    k_scr[...] = k * lax.rsqrt(_dot_sel_rhs(k * k, ones_bd, 2) + EPS)
    v_scr[...] = qkv[:, 2 * gw:3 * gw]

    ab = ab_ref[0]
    g_all = -jnp.exp(alog_ref[...]) * _softplus(ab + dtb_ref[...])
    lane = lax.broadcasted_iota(jnp.int32, ab.shape, 1)
    gb = jnp.where(lane < DN_HEADS, g_all, _sigmoid(ab))
    er = lax.broadcasted_iota(jnp.int32, (AB_W, 2 * gw), 0)
    ec = lax.broadcasted_iota(jnp.int32, (AB_W, 2 * gw), 1)
    expand = (er == jnp.where(ec < gw, ec // hd, DN_HEADS + (ec - gw) // hd)).astype(BF16)
    gbx = _dot_sel_rhs(gb, expand, 3)
    g_scr[...] = gbx[:, 0:gw]
    b_scr[...] = gbx[:, gw:2 * gw]

    ci = lax.broadcasted_iota(jnp.int32, (CHUNK, CHUNK), 0)
    cj = lax.broadcasted_iota(jnp.int32, (CHUNK, CHUNK), 1)
    causal = ci >= cj
    strict = ci > cj
    tri = causal.astype(BF16)
    n_sub = min(DN_UNROLL, tl // CHUNK)
    heads = [slice(h * hd, (h + 1) * hd) for h in range(DN_HEADS)]

    def chunk_group(c, carry):
        pre = []
        lmats, decays = [], []
        for s in range(n_sub):
            rows = pl.ds(pl.multiple_of((c * n_sub + s) * CHUNK, CHUNK), CHUNK)
            qc, kc, vc, bc = q_scr[rows, :], k_scr[rows, :], v_scr[rows, :], b_scr[rows, :]
            gcb = _dot_sel_lhs(tri, g_scr[rows, :], 3)
            eg = jnp.exp(gcb)
            glast = gcb[CHUNK - 1:CHUNK, :]
            kb = kc * bc
            pre.append(dict(rows=rows, qc=qc, kc=kc, kb=kb, qd=qc * eg, ku=kc * jnp.exp(glast - gcb),
                            wr=kb * eg, vb=vc * bc, eglast=jnp.exp(glast)))
            for hs in heads:
                gh = gcb[:, hs]
                diff = gh - gh.T
                decay = jnp.where(causal, jnp.exp(jnp.where(causal, diff, 0.0)), 0.0)
                kk = _dot_nt(kb[:, hs].astype(BF16), kc[:, hs].astype(BF16))
                decays.append(decay)
                lmats.append(jnp.where(strict, kk * decay, 0.0))
        tinvs = _unit_lower_inverses(lmats)
        sols, attns = [], []
        for s in range(n_sub):
            p = pre[s]
            for h, hs in enumerate(heads):
                i = s * DN_HEADS + h
                sols.append(_dot3(tinvs[i], jnp.concatenate([p["vb"][:, hs], p["wr"][:, hs]], axis=1)))
                attns.append((_dot_nt(p["qc"][:, hs].astype(BF16), p["kc"][:, hs].astype(BF16)) * decays[i]).astype(BF16))
        states = [s_ref[0, h] for h in range(DN_HEADS)]
        for s in range(n_sub):
            p = pre[s]
            outs = []
            for h, hs in enumerate(heads):
                i = s * DN_HEADS + h
                s_bf = states[h].astype(BF16)
                v_new = (sols[i][:, :hd] - _dot(sols[i][:, hd:].astype(BF16), s_bf)).astype(BF16)
                outs.append(_dot(p["qd"][:, hs].astype(BF16), s_bf) + _dot(attns[i], v_new))
                states[h] = states[h] * p["eglast"][:, hs] + _dot_tn(p["ku"][:, hs].astype(BF16), v_new)
            o_scr[p["rows"], :] = jnp.concatenate(outs, axis=1)
        for h in range(DN_HEADS):
            s_ref[0, h] = states[h]
        return carry

    lax.fori_loop(0, tl // (CHUNK * n_sub), chunk_group, 0)

    o = o_scr[...]
    ms = _dot_sel_rhs(o * o, ones_bd, 2) * (1.0 / hd)
    gate = gate_ref[0]
    y_ref[0] = (o * lax.rsqrt(ms + EPS) * ng_ref[...] * (gate * _sigmoid(gate))).astype(y_ref.dtype)


def _dn_mixer(u_qkv, u_gate, u_ab, prev, s0, conv_w, a_log, dt_bias, norm_g):
    b, l, _ = u_qkv.shape
    assert l % CHUNK == 0
    tl = min(l, ROW_TILE)
    assert (tl // CHUNK) % min(DN_UNROLL, tl // CHUNK) == 0
    gw3 = 3 * GROUP_W
    prev_p = jnp.pad(prev, ((0, 0), (DN_PAD - (DN_CONV - 1), 0), (0, 0)))
    alog_p = jnp.pad(a_log, (0, AB_W - DN_HEADS)).reshape(1, AB_W)
    dtb_p = jnp.pad(dt_bias, (0, AB_W - DN_HEADS)).reshape(1, AB_W)
    ng_p = jnp.tile(norm_g, DN_HEADS).reshape(1, GROUP_W)
    state_spec = pl.BlockSpec((1, DN_HEADS, DN_HEAD_DIM, DN_HEAD_DIM), lambda i, t: (i, 0, 0, 0))
    y, s_new, tail = pl.pallas_call(
        _dn_kernel,
        grid=(b, l // tl),
        in_specs=[pl.BlockSpec((1, tl, gw3), lambda i, t: (i, t, 0)),
                  pl.BlockSpec((1, tl, GROUP_W), lambda i, t: (i, t, 0)),
                  pl.BlockSpec((1, tl, AB_W), lambda i, t: (i, t, 0)),
                  pl.BlockSpec((1, DN_PAD, gw3), lambda i, t: (i, 0, 0)),
                  state_spec,
                  _const_spec((DN_CONV, gw3)), _const_spec((1, AB_W)), _const_spec((1, AB_W)),
                  _const_spec((1, GROUP_W))],
        out_specs=[pl.BlockSpec((1, tl, GROUP_W), lambda i, t: (i, t, 0)),
                   state_spec,
                   pl.BlockSpec((1, DN_PAD, gw3), lambda i, t: (i, 0, 0))],
        out_shape=[jax.ShapeDtypeStruct((b, l, GROUP_W), BF16),
                   jax.ShapeDtypeStruct((b, DN_HEADS, DN_HEAD_DIM, DN_HEAD_DIM), F32),
                   jax.ShapeDtypeStruct((b, DN_PAD, gw3), F32)],
        scratch_shapes=[pltpu.VMEM((DN_PAD + tl, gw3), F32)] + [pltpu.VMEM((tl, GROUP_W), F32)] * 6,
        compiler_params=_cparams("arbitrary", "arbitrary"),
        name="deltanet_mixer",
    )(u_qkv, u_gate, u_ab, prev_p, s0, conv_w, alog_p, dtb_p, ng_p)
    return y, s_new, tail[:, DN_PAD - (DN_CONV - 1):]


def _sb_prep_kernel(u_ref, qg_ref, kg_ref, q_ref, kt_ref, vb_ref, k_ref, v_ref):
    gw = GROUP_W
    u = u_ref[0]
    ones_bd = _head_ones(gw, BF16)
    q = u[:, 0:gw]
    k = u[:, gw:2 * gw]
    v = u[:, 2 * gw:3 * gw]
    inv_d = 1.0 / SB_HEAD_DIM
    qn = q * lax.rsqrt(_dot_sel_rhs(q * q, ones_bd, 2) * inv_d + EPS) * qg_ref[...]
    kn = k * lax.rsqrt(_dot_sel_rhs(k * k, ones_bd, 2) * inv_d + EPS) * kg_ref[...]
    q_ref[0] = (qn * (SB_HEAD_DIM ** -0.5)).astype(BF16)
    kt_ref[0] = kn.T.astype(BF16)
    vb_ref[0] = v.astype(BF16)
    for h in range(SB_HEADS):
        hs = slice(h * SB_HEAD_DIM, (h + 1) * SB_HEAD_DIM)
        k_ref[0, h] = kn[:, hs]
        v_ref[0, h] = v[:, hs]


def _sb_prep(u_sb, q_norm, k_norm):
    b, l, _ = u_sb.shape
    tl = min(l, ROW_TILE)
    head_spec = pl.BlockSpec((1, SB_HEADS, tl, SB_HEAD_DIM), lambda i, t: (i, 0, t, 0))
    row_spec = pl.BlockSpec((1, tl, GROUP_W), lambda i, t: (i, t, 0))
    head_shape = jax.ShapeDtypeStruct((b, SB_HEADS, l, SB_HEAD_DIM), F32)
    qg = jnp.tile(q_norm, (1, SB_HEADS))
    kg = jnp.tile(k_norm, (1, SB_HEADS))
    return pl.pallas_call(
        _sb_prep_kernel,
        grid=(b, l // tl),
        in_specs=[pl.BlockSpec((1, tl, 3 * GROUP_W), lambda i, t: (i, t, 0)),
                  _const_spec((1, GROUP_W)), _const_spec((1, GROUP_W))],
        out_specs=[row_spec, pl.BlockSpec((1, GROUP_W, tl), lambda i, t: (i, 0, t)), row_spec, head_spec, head_spec],
        out_shape=[jax.ShapeDtypeStruct((b, l, GROUP_W), BF16), jax.ShapeDtypeStruct((b, GROUP_W, l), BF16),
                   jax.ShapeDtypeStruct((b, l, GROUP_W), BF16), head_shape, head_shape],
        compiler_params=_cparams("arbitrary", "arbitrary"),
        name="sb_prep",
    )(u_sb, qg, kg)


def _sb_kernel(q_ref, kt_ref, v_ref, o_ref, later_scr, acc_scr, *, q_off, qb, kb):
    gw = GROUP_W
    q0 = q_off + pl.program_id(1) * qb
    nk = (q0 + qb - 1) // kb + 1
    head_of_lane = lax.broadcasted_iota(jnp.int32, (qb, gw), 1) // SB_HEAD_DIM
    q = q_ref[0].astype(F32)
    qh = [jnp.where(head_of_lane == h, q, 0.0).astype(BF16) for h in range(SB_HEADS)]
    r = lax.broadcasted_iota(jnp.int32, (kb, kb), 0)
    c = lax.broadcasted_iota(jnp.int32, (kb, kb), 1)
    suffix = (r >= c).astype(BF16)
    qpos = q0 + lax.broadcasted_iota(jnp.int32, (qb, kb), 0)
    kcol = lax.broadcasted_iota(jnp.int32, (qb, kb), 1)
    later_scr[...] = jnp.zeros_like(later_scr)
    acc_scr[...] = jnp.zeros_like(acc_scr)

    def tile(j, masked):
        ks = pl.ds(pl.multiple_of(j * kb, kb), kb)
        kt = kt_ref[0, :, ks]
        v = v_ref[0, ks, :]
        zs = [_dot(qh[h], kt) for h in range(SB_HEADS)]
        if masked:
            visible = j * kb + kcol < qpos
            zs = [jnp.where(visible, z, SB_NEG) for z in zs]
        sps = [jnp.maximum(z, 0.0) + jnp.log(1.0 + jnp.exp(-jnp.abs(z))) for z in zs]
        css = [_dot(sp.astype(BF16), suffix) for sp in sps]
        pv = None
        for h in range(SB_HEADS):
            later = later_scr[h]
            a = jnp.exp(zs[h] - css[h] - jnp.concatenate([later] * (kb // V7X_LANES), axis=1))
            pvh = _dot(a.astype(BF16), v)
            pv = pvh if pv is None else jnp.where(head_of_lane >= h, pvh, pv)
            later_scr[h] = later + jnp.broadcast_to(css[h][:, 0:1], later.shape)
        acc_scr[...] += pv

    tile(nk - 1, True)

    def body(i, carry):
        tile(nk - 2 - i, False)
        return carry

    lax.fori_loop(0, nk - 1, body, 0)
    o_ref[0] = acc_scr[...].astype(o_ref.dtype)


def _sb_attention(q, kt_all, v_all, q_off):
    b, lq, gw = q.shape
    lk = v_all.shape[1]
    qb = min(SB_BLOCK, lq)
    kb = SB_BLOCK
    assert lq % qb == 0 and lk % kb == 0 and q_off % kb == 0 and kb % qb == 0
    assert lk >= q_off + lq
    return pl.pallas_call(
        functools.partial(_sb_kernel, q_off=q_off, qb=qb, kb=kb),
        grid=(b, lq // qb),
        in_specs=[pl.BlockSpec((1, qb, gw), lambda i, t: (i, t, 0)),
                  pl.BlockSpec((1, gw, lk), lambda i, t: (i, 0, 0)),
                  pl.BlockSpec((1, lk, gw), lambda i, t: (i, 0, 0))],
        out_specs=pl.BlockSpec((1, qb, gw), lambda i, t: (i, t, 0)),
        out_shape=jax.ShapeDtypeStruct((b, lq, gw), BF16),
        scratch_shapes=[pltpu.VMEM((SB_HEADS, qb, V7X_LANES), F32), pltpu.VMEM((qb, gw), F32)],
        compiler_params=_cparams("arbitrary", "arbitrary"),
        name="sb_attention",
    )(q, kt_all, v_all)


def _sb_mixer(u_sb, k_prev, v_prev, q_norm, k_norm):
    q, kt, vb, k_new, v_new = _sb_prep(u_sb, q_norm, k_norm)
    b, heads, past, d = k_prev.shape
    lq = u_sb.shape[1]
    pad = (-(past + lq)) % SB_BLOCK
    if past or pad:
        kt_prev = jnp.swapaxes(k_prev, 2, 3).reshape(b, heads * d, past).astype(BF16)
        v_prev_rows = jnp.swapaxes(v_prev, 1, 2).reshape(b, past, heads * d).astype(BF16)
        kt = jnp.pad(jnp.concatenate([kt_prev, kt], axis=2), ((0, 0), (0, 0), (0, pad)))
        vb = jnp.pad(jnp.concatenate([v_prev_rows, vb], axis=1), ((0, 0), (0, pad), (0, 0)))
    return _sb_attention(q, kt, vb, past), k_new, v_new


def _out_ffn_kernel(x_ref, yc_ref, yd_ref, yp_ref, ys_ref, mod_ref, g_ref, wo_ref, wg_ref, wu_ref, wd_ref, o_ref):
    tb, tl, d = x_ref.shape
    m = tb * tl
    gw = GROUP_W
    ycat = jnp.concatenate([r[...].reshape(m, gw) for r in (yc_ref, yd_ref, yp_ref, ys_ref)], axis=1)
    mix = _dot(ycat, wo_ref[...])
    x1 = x_ref[...] + mod_ref[:, 2:3, :] * mix.reshape(tb, tl, d)
    y = x1 * lax.rsqrt(jnp.mean(x1 * x1, axis=-1, keepdims=True) + EPS) * g_ref[...]
    h2 = (y * (1.0 + mod_ref[:, 4:5, :]) + mod_ref[:, 3:4, :]).reshape(m, d).astype(BF16)
    ff = D_FF // FF_SPLIT
    acc = jnp.zeros((m, d), F32)
    for c in range(FF_SPLIT):
        gate = _dot(h2, wg_ref[:, c * ff:(c + 1) * ff])
        up = _dot(h2, wu_ref[:, c * ff:(c + 1) * ff])
        act = (gate * _sigmoid(gate) * up).astype(BF16)
        acc = acc + _dot(act, wd_ref[c * ff:(c + 1) * ff, :])
    o_ref[...] = x1 + mod_ref[:, 5:6, :] * acc.reshape(tb, tl, d)


def _out_ffn(x, y_conv, y_dn, y_pool, y_sb, mod, norm_g, w_out, w_gate, w_up, w_down, layer):
    b, l, d = x.shape
    tb, tl = _row_tiles(b, l)
    assert (D_FF // FF_SPLIT) % V7X_LANES == 0
    row_spec = lambda n: pl.BlockSpec((tb, tl, n), lambda i, t: (i, t, 0))
    return pl.pallas_call(
        _out_ffn_kernel,
        grid=(b // tb, l // tl),
        in_specs=[row_spec(d), row_spec(GROUP_W), row_spec(GROUP_W), row_spec(GROUP_W), row_spec(GROUP_W),
                  pl.BlockSpec((tb, N_MOD, d), lambda i, t: (i, 0, 0)),
                  _const_spec((1, d)),
                  _const_spec((d, d), layer), _const_spec((d, D_FF), layer),
                  _const_spec((d, D_FF), layer), _const_spec((D_FF, d), layer)],
        out_specs=row_spec(d),
        out_shape=jax.ShapeDtypeStruct((b, l, d), F32),
        compiler_params=_cparams("arbitrary", "arbitrary"),
        name="out_ffn",
    )(x, y_conv, y_dn, y_pool, y_sb, mod, norm_g, w_out, w_gate, w_up, w_down)


def _pack_w_in(w_in):
    c0 = 2 * GROUP_W + 3 * GROUP_W + GROUP_W
    ab = w_in[..., c0:c0 + 2 * DN_HEADS]
    ab = jnp.pad(ab, ((0, 0), (0, 0), (0, AB_W - 2 * DN_HEADS)))
    return jnp.concatenate([w_in[..., :c0], ab, w_in[..., c0 + 2 * DN_HEADS:]], axis=-1).astype(BF16)


def _pool_block_diag(pool_w):
    g, n, _ = pool_w.shape
    out = jnp.zeros((g * n, g * n), pool_w.dtype)
    for i in range(g):
        out = out.at[i * n:(i + 1) * n, i * n:(i + 1) * n].set(pool_w[i])
    return out.astype(BF16)


def _trunk_layer(x, mod, pos0, conv_prev, dn_s0, dn_conv_prev, pool_prev, k_prev, v_prev, p, layer):
    u_conv, u_qkv, u_gate, u_ab, u_pool, u_sb = _in_proj(x, mod, p["norm_mix"], p["w_in"], layer)
    y_conv, conv_new = _conv_mixer(u_conv, conv_prev, p["conv_dw_w"], p["conv_dw_b"], p["conv_ln_g"], p["conv_ln_b"])
    y_dn, s_new, dn_conv_new = _dn_mixer(u_qkv, u_gate, u_ab, dn_conv_prev, dn_s0, p["dn_conv_w"],
                                         p["dn_a_log"], p["dn_dt_bias"], p["dn_norm_g"])
    y_pool, pool_new = _pool_mixer(u_pool, pool_prev, p["pool_w"], p["pool_scale"], pos0)
    y_sb, k_new, v_new = _sb_mixer(u_sb, k_prev, v_prev, p["sb_q_norm"], p["sb_k_norm"])
    x = _out_ffn(x, y_conv, y_dn, y_pool, y_sb, mod, p["norm_ffn"], p["w_out"], p["ffn_w_gate"],
                 p["ffn_w_up"], p["ffn_w_down"], layer)
    return x, conv_new, s_new, dn_conv_new, pool_new, k_new, v_new


def kernel(x_prompt, x_sample, c_prompt, c_sample, cache_conv, state_dn, cache_dn_conv, cache_pool, cache_sb_k, cache_sb_v, w_ada, b_ada, norm_mix, norm_ffn, w_in, w_out, conv_dw_w, conv_dw_b, conv_ln_g, conv_ln_b, dn_conv_w, dn_a_log, dn_dt_bias, dn_norm_g, pool_w, pool_scale, sb_q_norm, sb_k_norm, ffn_w_gate, ffn_w_up, ffn_w_down):
    depth = w_ada.shape[0]
    bp = x_prompt.shape[0]
    bs = x_sample.shape[0]
    past = cache_sb_k.shape[3]
    dt = x_prompt.dtype

    mod_all = _ada_modulation(jnp.concatenate([c_prompt, c_sample], axis=0), w_ada, b_ada)
    mod_all = mod_all.reshape(depth, bp + bs, N_MOD, D_MODEL)
    w_in_p = _pack_w_in(w_in)
    w_out_b, w_gate_b, w_up_b, w_down_b = (w.astype(BF16) for w in (w_out, ffn_w_gate, ffn_w_up, ffn_w_down))

    zeros_p = dict(
        conv=jnp.zeros((bp, CONV_W - 1, GROUP_W), dt),
        s0=jnp.zeros((bp, DN_HEADS, DN_HEAD_DIM, DN_HEAD_DIM), F32),
        dn_conv=jnp.zeros((bp, DN_CONV - 1, 3 * GROUP_W), dt),
        pool=jnp.zeros((bp, POOL_PREFIX, GROUP_W), dt),
        kv=jnp.zeros((bp, SB_HEADS, 0, SB_HEAD_DIM), dt))

    xp, xs = x_prompt, x_sample
    new_p = [[] for _ in range(6)]
    new_s = [[] for _ in range(6)]
    for l in range(depth):
        row = lambda a: a[l].reshape(1, -1)
        p = dict(norm_mix=row(norm_mix), norm_ffn=row(norm_ffn), w_in=w_in_p, w_out=w_out_b,
                 conv_dw_w=conv_dw_w[l], conv_dw_b=row(conv_dw_b), conv_ln_g=row(conv_ln_g), conv_ln_b=row(conv_ln_b),
                 dn_conv_w=dn_conv_w[l], dn_a_log=dn_a_log[l], dn_dt_bias=dn_dt_bias[l], dn_norm_g=dn_norm_g[l],
                 pool_w=_pool_block_diag(pool_w[l]), pool_scale=row(pool_scale),
                 sb_q_norm=row(sb_q_norm), sb_k_norm=row(sb_k_norm),
                 ffn_w_gate=w_gate_b, ffn_w_up=w_up_b, ffn_w_down=w_down_b)
        xp, *sp = _trunk_layer(xp, mod_all[l, :bp], 0, zeros_p["conv"], zeros_p["s0"], zeros_p["dn_conv"],
                               zeros_p["pool"], zeros_p["kv"], zeros_p["kv"], p, l)
        xs, *ss = _trunk_layer(xs, mod_all[l, bp:], past, cache_conv[l], state_dn[l], cache_dn_conv[l],
                               cache_pool[l], cache_sb_k[l], cache_sb_v[l], p, l)
        for i in range(6):
            new_p[i].append(sp[i])
            new_s[i].append(ss[i])
    conv_p, dn_p, dnconv_p, pool_p, sbk_p, sbv_p = [jnp.stack(a) for a in new_p]
    conv_s, dn_s, dnconv_s, pool_s, sbk_s, sbv_s = [jnp.stack(a) for a in new_s]
    return (xp, xs, conv_p, conv_s, dn_p, dn_s, dnconv_p, dnconv_s, pool_p, pool_s, sbk_p, sbk_s, sbv_p, sbv_s)
```

```python
import functools

import jax
import jax.numpy as jnp
from jax import lax
from jax.experimental import pallas as pl
from jax.experimental.pallas import tpu as pltpu

D_MODEL = 1024
GROUP_W = 256
CONV_W = 31
DN_HEADS = 4
DN_HEAD_DIM = 64
DN_CONV = 4
CHUNK = 64
POOL_PREFIX = 15
SB_HEADS = 4
SB_HEAD_DIM = 64
SB_NEG = -1e30
D_FF = 2816
N_MOD = 6
EPS = 1e-6

V7X_LANES = 128
V7X_SUBLANES = 8
V7X_MXU_DIM = 256
V7X_VMEM_BYTES = 64 * 1024 * 1024
VMEM_LIMIT = V7X_VMEM_BYTES * 7 // 8

ROW_TILE = 512
FF_SPLIT = 2
CONV_PAD = 32
DN_PAD = 8
POOL_PAD = 16
AB_W = V7X_LANES
DN_UNROLL = 4
SB_BLOCK = V7X_MXU_DIM
SB_UNROLL = 2
LOG2_E = 1.4426950408889634
SQ_PARTS = 1
IN_WIDTHS = (2 * GROUP_W, 3 * GROUP_W, GROUP_W, AB_W, GROUP_W, 3 * GROUP_W)

F32 = jnp.float32
BF16 = jnp.bfloat16


def _cparams(*sem):
    return pltpu.CompilerParams(dimension_semantics=sem, vmem_limit_bytes=VMEM_LIMIT)


def _dot(a, b):
    return jnp.dot(a, b, preferred_element_type=F32)


def _dot_nt(a, b):
    return lax.dot_general(a, b, (((1,), (1,)), ((), ())), preferred_element_type=F32)


def _dot_tn(a, b):
    return lax.dot_general(a, b, (((0,), (0,)), ((), ())), preferred_element_type=F32)


def _split(x, parts):
    out = []
    for _ in range(parts):
        p = x.astype(BF16)
        out.append(p)
        x = x - p.astype(F32)
    return out


def _dot_sel_rhs(a, sel, parts):
    pieces = _split(a, parts)
    out = _dot(pieces[0], sel)
    for p in pieces[1:]:
        out = out + _dot(p, sel)
    return out


def _dot_sel_lhs(sel, b, parts):
    pieces = _split(b, parts)
    out = _dot(sel, pieces[0])
    for p in pieces[1:]:
        out = out + _dot(sel, p)
    return out


def _sigmoid(x):
    return 1.0 / (1.0 + jnp.exp(-x))


def _softplus(x):
    return jnp.maximum(x, 0.0) + jnp.log1p(jnp.exp(-jnp.abs(x)))


def _row_tiles(batch, length):
    if length >= ROW_TILE:
        assert length % ROW_TILE == 0
        return 1, ROW_TILE
    tb = max(1, min(batch, ROW_TILE // length))
    while batch % tb:
        tb -= 1
    return tb, length


def _const_spec(shape, layer=None):
    nd = len(shape)
    if layer is None:
        return pl.BlockSpec(shape, lambda *_: (0,) * nd, pipeline_mode=pl.Buffered(1))
    return pl.BlockSpec((None,) + shape, lambda *_: (layer,) + (0,) * nd, pipeline_mode=pl.Buffered(1))


def _head_ones(n, dtype):
    r = lax.broadcasted_iota(jnp.int32, (n, n), 0) // DN_HEAD_DIM
    c = lax.broadcasted_iota(jnp.int32, (n, n), 1) // DN_HEAD_DIM
    return (r == c).astype(dtype)


def _ada_kernel(c_ref, w_ref, b_ref, o_ref):
    c = c_ref[...]
    a = (c * _sigmoid(c)).astype(BF16)
    o_ref[0] = _dot(a, w_ref[0].astype(BF16)) + b_ref[0]


def _ada_modulation(c_all, w_ada, b_ada):
    depth, d, n = w_ada.shape
    nb = c_all.shape[0]
    tn = 1536
    assert n % tn == 0
    return pl.pallas_call(
        _ada_kernel,
        grid=(depth, n // tn),
        in_specs=[pl.BlockSpec((nb, d), lambda l, j: (0, 0)),
                  pl.BlockSpec((1, d, tn), lambda l, j: (l, 0, j)),
                  pl.BlockSpec((1, 1, tn), lambda l, j: (l, 0, j))],
        out_specs=pl.BlockSpec((1, nb, tn), lambda l, j: (l, 0, j)),
        out_shape=jax.ShapeDtypeStruct((depth, nb, n), F32),
        compiler_params=_cparams("arbitrary", "arbitrary"),
        name="ada_modulation",
    )(c_all, w_ada, b_ada.reshape(depth, 1, n))


def _in_proj_kernel(x_ref, mod_ref, g_ref, w_ref, *out_refs):
    tb, tl, d = x_ref.shape
    x = x_ref[...]
    y = x * lax.rsqrt(jnp.mean(x * x, axis=-1, keepdims=True) + EPS) * g_ref[...]
    h = y * (1.0 + mod_ref[:, 1:2, :]) + mod_ref[:, 0:1, :]
    hb = h.reshape(tb * tl, d).astype(BF16)
    c0 = 0
    for o_ref in out_refs:
        n = o_ref.shape[-1]
        o_ref[...] = _dot(hb, w_ref[:, c0:c0 + n]).reshape(tb, tl, n)
        c0 += n


def _in_proj(x, mod, norm_g, w_in_all, layer):
    b, l, d = x.shape
    tb, tl = _row_tiles(b, l)
    n_all = w_in_all.shape[-1]
    return pl.pallas_call(
        _in_proj_kernel,
        grid=(b // tb, l // tl),
        in_specs=[pl.BlockSpec((tb, tl, d), lambda i, t: (i, t, 0)),
                  pl.BlockSpec((tb, N_MOD, d), lambda i, t: (i, 0, 0)),
                  _const_spec((1, d)),
                  _const_spec((d, n_all), layer)],
        out_specs=[pl.BlockSpec((tb, tl, n), lambda i, t: (i, t, 0)) for n in IN_WIDTHS],
        out_shape=[jax.ShapeDtypeStruct((b, l, n), F32) for n in IN_WIDTHS],
        compiler_params=_cparams("arbitrary", "arbitrary"),
        name="in_proj",
    )(x, mod, norm_g, w_in_all)


def _conv_kernel(u_ref, prev_ref, w_ref, b_ref, g_ref, beta_ref, y_ref, tail_ref, full_scr):
    tl = u_ref.shape[1]

    @pl.when(pl.program_id(1) == 0)
    def _():
        full_scr[0:CONV_PAD, :] = prev_ref[0]

    u = u_ref[0]
    full_scr[CONV_PAD:CONV_PAD + tl, :] = u[:, :GROUP_W] * _sigmoid(u[:, GROUP_W:])
    sub = min(tl, 64)
    first = CONV_PAD - (CONV_W - 1)
    for r in range(0, tl, sub):
        acc = jnp.zeros((sub, GROUP_W), F32)
        for k in range(CONV_W):
            acc = acc + w_ref[k:k + 1, :] * full_scr[first + r + k:first + r + k + sub, :]
        y = acc + b_ref[...]
        mu = jnp.mean(y, axis=-1, keepdims=True)
        yc = y - mu
        var = jnp.mean(yc * yc, axis=-1, keepdims=True)
        yn = yc * lax.rsqrt(var + EPS) * g_ref[...] + beta_ref[...]
        y_ref[0, r:r + sub, :] = (yn * _sigmoid(yn)).astype(y_ref.dtype)
    tail = full_scr[tl:tl + CONV_PAD, :]
    tail_ref[0] = tail
    full_scr[0:CONV_PAD, :] = tail


def _conv_mixer(u_conv, prev, w, bias, ln_g, ln_b):
    b, l, _ = u_conv.shape
    tl = min(l, ROW_TILE)
    prev_p = jnp.pad(prev, ((0, 0), (CONV_PAD - (CONV_W - 1), 0), (0, 0)))
    y, tail = pl.pallas_call(
        _conv_kernel,
        grid=(b, l // tl),
        in_specs=[pl.BlockSpec((1, tl, 2 * GROUP_W), lambda i, t: (i, t, 0)),
                  pl.BlockSpec((1, CONV_PAD, GROUP_W), lambda i, t: (i, 0, 0)),
                  _const_spec((CONV_W, GROUP_W)),
                  _const_spec((1, GROUP_W)), _const_spec((1, GROUP_W)), _const_spec((1, GROUP_W))],
        out_specs=[pl.BlockSpec((1, tl, GROUP_W), lambda i, t: (i, t, 0)),
                   pl.BlockSpec((1, CONV_PAD, GROUP_W), lambda i, t: (i, 0, 0))],
        out_shape=[jax.ShapeDtypeStruct((b, l, GROUP_W), BF16),
                   jax.ShapeDtypeStruct((b, CONV_PAD, GROUP_W), F32)],
        scratch_shapes=[pltpu.VMEM((CONV_PAD + tl, GROUP_W), F32)],
        compiler_params=_cparams("arbitrary", "arbitrary"),
        name="conv_mixer",
    )(u_conv, prev_p, w, bias, ln_g, ln_b)
    return y, tail[:, CONV_PAD - (CONV_W - 1):]


def _pool_kernel(u_ref, prev_ref, w_ref, scale_ref, y_ref, tail_ref, full_scr, *, pos0):
    tl = u_ref.shape[1]
    t = pl.program_id(1)

    @pl.when(t == 0)
    def _():
        full_scr[0:POOL_PAD, :] = prev_ref[0]

    cur = u_ref[0]
    full_scr[POOL_PAD:POOL_PAD + tl, :] = cur
    lane = lax.broadcasted_iota(jnp.int32, (tl, GROUP_W), 1)
    row = lax.broadcasted_iota(jnp.int32, (tl, GROUP_W), 0)
    group = lane // (GROUP_W // 4)
    acc = cur
    win = None
    for i in range(1, 16):
        acc = acc + full_scr[POOL_PAD - i:POOL_PAD - i + tl, :]
        if i in (1, 3, 7, 15):
            gi = (1, 3, 7, 15).index(i)
            win = acc if win is None else jnp.where(group >= gi, acc, win)
    width = jnp.left_shift(2, group)
    cnt = jnp.minimum(pos0 + t * tl + row + 1, width).astype(F32)
    pooled = win / cnt - cur
    y = _dot(pooled.astype(BF16), w_ref[...]) * scale_ref[...]
    y_ref[0] = y.astype(y_ref.dtype)
    tail = full_scr[tl:tl + POOL_PAD, :]
    tail_ref[0] = tail
    full_scr[0:POOL_PAD, :] = tail


def _pool_mixer(u_pool, prev, w_bd, scale, pos0):
    b, l, _ = u_pool.shape
    tl = min(l, ROW_TILE)
    prev_p = jnp.pad(prev, ((0, 0), (POOL_PAD - POOL_PREFIX, 0), (0, 0)))
    y, tail = pl.pallas_call(
        functools.partial(_pool_kernel, pos0=pos0),
        grid=(b, l // tl),
        in_specs=[pl.BlockSpec((1, tl, GROUP_W), lambda i, t: (i, t, 0)),
                  pl.BlockSpec((1, POOL_PAD, GROUP_W), lambda i, t: (i, 0, 0)),
                  _const_spec((GROUP_W, GROUP_W)), _const_spec((1, GROUP_W))],
        out_specs=[pl.BlockSpec((1, tl, GROUP_W), lambda i, t: (i, t, 0)),
                   pl.BlockSpec((1, POOL_PAD, GROUP_W), lambda i, t: (i, 0, 0))],
        out_shape=[jax.ShapeDtypeStruct((b, l, GROUP_W), BF16),
                   jax.ShapeDtypeStruct((b, POOL_PAD, GROUP_W), F32)],
        scratch_shapes=[pltpu.VMEM((POOL_PAD + tl, GROUP_W), F32)],
        compiler_params=_cparams("arbitrary", "arbitrary"),
        name="pool_mixer",
    )(u_pool, prev_p, w_bd, scale)
    return y, tail[:, POOL_PAD - POOL_PREFIX:]


def _unit_lower_inverses(lmats):
    c = lmats[0].shape[0]
    eye = (lax.broadcasted_iota(jnp.int32, (c, c), 0) == lax.broadcasted_iota(jnp.int32, (c, c), 1)).astype(F32)
    xs = [eye - m for m in lmats]
    ps = []
    for m in lmats:
        mh, ml = _split(m, 2)
        r1 = _dot(jnp.concatenate([mh, ml], axis=0), mh)
        ps.append(r1[0:c] + r1[c:2 * c] + _dot(mh, ml))
    span = 2
    while True:
        span *= 2
        last = span >= c
        nxt_x, nxt_p = [], []
        for x, p in zip(xs, ps):
            xh, xl = _split(x, 2)
            ph, pl_ = _split(p, 2)
            if last:
                r1 = _dot(jnp.concatenate([xh, xl], axis=0), ph)
                nxt_x.append(x + (r1[0:c] + r1[c:2 * c] + _dot(xh, pl_)))
            else:
                r1 = _dot(jnp.concatenate([xh, xl, ph, pl_], axis=0), ph)
                r2 = _dot(jnp.concatenate([xh, ph], axis=0), pl_)
                nxt_x.append(x + (r1[0:c] + r1[c:2 * c] + r2[0:c]))
                nxt_p.append(r1[2 * c:3 * c] + r1[3 * c:4 * c] + r2[c:2 * c])
        xs, ps = nxt_x, nxt_p
        if last:
            return xs


def _dn_kernel(qkv_ref, gate_ref, ab_ref, prev_ref, s0_ref, cw_ref, alog_ref, dtb_ref, ng_ref,
               y_ref, s_ref, tail_ref, full_scr, q_scr, k_scr, v_scr, g_scr, b_scr, o_scr):
    tl = qkv_ref.shape[1]
    gw = GROUP_W
    hd = DN_HEAD_DIM

    @pl.when(pl.program_id(1) == 0)
    def _():
        full_scr[0:DN_PAD, :] = prev_ref[0]
        s_ref[0] = s0_ref[0]

    full_scr[DN_PAD:DN_PAD + tl, :] = qkv_ref[0]
    first = DN_PAD - (DN_CONV - 1)
    acc = cw_ref[0:1, :] * full_scr[first:first + tl, :]
    for k in range(1, DN_CONV):
        acc = acc + cw_ref[k:k + 1, :] * full_scr[first + k:first + k + tl, :]
    qkv = acc * _sigmoid(acc)
    tail = full_scr[tl:tl + DN_PAD, :]
    tail_ref[0] = tail
    full_scr[0:DN_PAD, :] = tail

    ones_bd = _head_ones(gw, BF16)
    q = qkv[:, 0:gw]
    k = qkv[:, gw:2 * gw]
    q_scr[...] = q * lax.rsqrt(_dot_sel_rhs(q * q, ones_bd, SQ_PARTS) + EPS) * (hd ** -0.5)
    k_scr[...] = k * lax.rsqrt(_dot_sel_rhs(k * k, ones_bd, SQ_PARTS) + EPS)
    v_scr[...] = qkv[:, 2 * gw:3 * gw]

    ab = ab_ref[0]
    g_all = -jnp.exp(alog_ref[...]) * _softplus(ab + dtb_ref[...])
    lane = lax.broadcasted_iota(jnp.int32, ab.shape, 1)
    gb = jnp.where(lane < DN_HEADS, g_all, _sigmoid(ab))
    er = lax.broadcasted_iota(jnp.int32, (AB_W, 2 * gw), 0)
    ec = lax.broadcasted_iota(jnp.int32, (AB_W, 2 * gw), 1)
    expand = (er == jnp.where(ec < gw, ec // hd, DN_HEADS + (ec - gw) // hd)).astype(BF16)
    gbx = _dot_sel_rhs(gb, expand, 3)
    g_scr[...] = gbx[:, 0:gw]
    b_scr[...] = gbx[:, gw:2 * gw]

    ci = lax.broadcasted_iota(jnp.int32, (CHUNK, CHUNK), 0)
    cj = lax.broadcasted_iota(jnp.int32, (CHUNK, CHUNK), 1)
    causal = ci >= cj
    strict = ci > cj
    tri = causal.astype(BF16)
    n_sub = min(DN_UNROLL, tl // CHUNK)
    heads = [slice(h * hd, (h + 1) * hd) for h in range(DN_HEADS)]

    def chunk_group(c, carry):
        pre = []
        lmats, attns = [], []
        for s in range(n_sub):
            rows = pl.ds(pl.multiple_of((c * n_sub + s) * CHUNK, CHUNK), CHUNK)
            qc, kc, vc, bc = q_scr[rows, :], k_scr[rows, :], v_scr[rows, :], b_scr[rows, :]
            gcb = _dot_sel_lhs(tri, g_scr[rows, :], 3)
            eg = jnp.exp(gcb)
            glast = gcb[CHUNK - 1:CHUNK, :]
            kb = kc * bc
            pre.append(dict(rows=rows, qc=qc, kc=kc, kb=kb, qd=qc * eg, ku=kc * jnp.exp(glast - gcb),
                            wr=kb * eg, vb=vc * bc, eglast=jnp.exp(glast)))
            for hs in heads:
                gh = gcb[:, hs]
                diff = gh - gh.T
                decay = jnp.where(causal, jnp.exp(jnp.where(causal, diff, 0.0)), 0.0)
                both = _dot_nt(jnp.concatenate([kb[:, hs], qc[:, hs]], axis=0).astype(BF16), kc[:, hs].astype(BF16))
                lmats.append(jnp.where(strict, both[0:CHUNK] * decay, 0.0))
                attns.append((both[CHUNK:2 * CHUNK] * decay).astype(BF16))
        tinvs = _unit_lower_inverses(lmats)
        sols = []
        for s in range(n_sub):
            p = pre[s]
            for h, hs in enumerate(heads):
                th, tlo = _split(tinvs[s * DN_HEADS + h], 2)
                rh, rlo = _split(jnp.concatenate([p["vb"][:, hs], p["wr"][:, hs]], axis=1), 2)
                r1 = _dot(jnp.concatenate([th, tlo], axis=0), rh)
                sols.append(r1[0:CHUNK] + r1[CHUNK:2 * CHUNK] + _dot(th, rlo))
        states = [s_ref[0, h] for h in range(DN_HEADS)]
        for s in range(n_sub):
            p = pre[s]
            outs = []
            for h, hs in enumerate(heads):
                i = s * DN_HEADS + h
                ws = _dot(jnp.concatenate([sols[i][:, hd:], p["qd"][:, hs]], axis=0).astype(BF16), states[h].astype(BF16))
                v_new = (sols[i][:, :hd] - ws[0:CHUNK]).astype(BF16)
                outs.append(ws[CHUNK:2 * CHUNK] + _dot(attns[i], v_new))
                states[h] = states[h] * p["eglast"][:, hs] + _dot_tn(p["ku"][:, hs].astype(BF16), v_new)
            o_scr[p["rows"], :] = jnp.concatenate(outs, axis=1)
        for h in range(DN_HEADS):
            s_ref[0, h] = states[h]
        return carry

    lax.fori_loop(0, tl // (CHUNK * n_sub), chunk_group, 0)

    o = o_scr[...]
    ms = _dot_sel_rhs(o * o, ones_bd, SQ_PARTS) * (1.0 / hd)
    gate = gate_ref[0]
    y_ref[0] = (o * lax.rsqrt(ms + EPS) * ng_ref[...] * (gate * _sigmoid(gate))).astype(y_ref.dtype)


def _dn_mixer(u_qkv, u_gate, u_ab, prev, s0, conv_w, a_log, dt_bias, norm_g):
    b, l, _ = u_qkv.shape
    assert l % CHUNK == 0
    tl = min(l, ROW_TILE)
    assert (tl // CHUNK) % min(DN_UNROLL, tl // CHUNK) == 0
    gw3 = 3 * GROUP_W
    prev_p = jnp.pad(prev, ((0, 0), (DN_PAD - (DN_CONV - 1), 0), (0, 0)))
    alog_p = jnp.pad(a_log, (0, AB_W - DN_HEADS)).reshape(1, AB_W)
    dtb_p = jnp.pad(dt_bias, (0, AB_W - DN_HEADS)).reshape(1, AB_W)
    ng_p = jnp.tile(norm_g, DN_HEADS).reshape(1, GROUP_W)
    state_spec = pl.BlockSpec((1, DN_HEADS, DN_HEAD_DIM, DN_HEAD_DIM), lambda i, t: (i, 0, 0, 0))
    y, s_new, tail = pl.pallas_call(
        _dn_kernel,
        grid=(b, l // tl),
        in_specs=[pl.BlockSpec((1, tl, gw3), lambda i, t: (i, t, 0)),
                  pl.BlockSpec((1, tl, GROUP_W), lambda i, t: (i, t, 0)),
                  pl.BlockSpec((1, tl, AB_W), lambda i, t: (i, t, 0)),
                  pl.BlockSpec((1, DN_PAD, gw3), lambda i, t: (i, 0, 0)),
                  state_spec,
                  _const_spec((DN_CONV, gw3)), _const_spec((1, AB_W)), _const_spec((1, AB_W)),
                  _const_spec((1, GROUP_W))],
        out_specs=[pl.BlockSpec((1, tl, GROUP_W), lambda i, t: (i, t, 0)),
                   state_spec,
                   pl.BlockSpec((1, DN_PAD, gw3), lambda i, t: (i, 0, 0))],
        out_shape=[jax.ShapeDtypeStruct((b, l, GROUP_W), BF16),
                   jax.ShapeDtypeStruct((b, DN_HEADS, DN_HEAD_DIM, DN_HEAD_DIM), F32),
                   jax.ShapeDtypeStruct((b, DN_PAD, gw3), F32)],
        scratch_shapes=[pltpu.VMEM((DN_PAD + tl, gw3), F32)] + [pltpu.VMEM((tl, GROUP_W), F32)] * 6,
        compiler_params=_cparams("arbitrary", "arbitrary"),
        name="deltanet_mixer",
    )(u_qkv, u_gate, u_ab, prev_p, s0, conv_w, alog_p, dtb_p, ng_p)
    return y, s_new, tail[:, DN_PAD - (DN_CONV - 1):]


def _sb_prep_kernel(u_ref, qg_ref, kg_ref, k_all_ref, v_all_ref, q_ref, kt_ref, vb_ref, k_ref, v_ref):
    del k_all_ref, v_all_ref
    gw = GROUP_W
    u = u_ref[0]
    ones_bd = _head_ones(gw, BF16)
    q = u[:, 0:gw]
    k = u[:, gw:2 * gw]
    v = u[:, 2 * gw:3 * gw]
    inv_d = 1.0 / SB_HEAD_DIM
    qn = q * lax.rsqrt(_dot_sel_rhs(q * q, ones_bd, SQ_PARTS) * inv_d + EPS) * qg_ref[...]
    kn = k * lax.rsqrt(_dot_sel_rhs(k * k, ones_bd, SQ_PARTS) * inv_d + EPS) * kg_ref[...]
    q_ref[0] = (qn * (SB_HEAD_DIM ** -0.5 * LOG2_E)).astype(BF16)
    kt_ref[0] = kn.T.astype(BF16)
    vb_ref[0] = v.astype(BF16)
    for h in range(SB_HEADS):
        hs = slice(h * SB_HEAD_DIM, (h + 1) * SB_HEAD_DIM)
        k_ref[0, h] = kn[:, hs]
        v_ref[0, h] = v[:, hs]


def _sb_prep(u_sb, q_norm, k_norm, k_all, v_all, layer):
    b, l, _ = u_sb.shape
    tl = min(l, ROW_TILE)
    head_spec = pl.BlockSpec((None, 1, SB_HEADS, tl, SB_HEAD_DIM), lambda i, t: (layer, i, 0, t, 0))
    row_spec = pl.BlockSpec((1, tl, GROUP_W), lambda i, t: (i, t, 0))
    head_shape = jax.ShapeDtypeStruct(k_all.shape, F32)
    assert k_all.shape[1:] == (b, SB_HEADS, l, SB_HEAD_DIM) and v_all.shape == k_all.shape
    qg = jnp.tile(q_norm, (1, SB_HEADS))
    kg = jnp.tile(k_norm, (1, SB_HEADS))
    return pl.pallas_call(
        _sb_prep_kernel,
        grid=(b, l // tl),
        in_specs=[pl.BlockSpec((1, tl, 3 * GROUP_W), lambda i, t: (i, t, 0)),
                  _const_spec((1, GROUP_W)), _const_spec((1, GROUP_W)),
                  pl.BlockSpec(memory_space=pl.ANY), pl.BlockSpec(memory_space=pl.ANY)],
        out_specs=[row_spec, pl.BlockSpec((1, GROUP_W, tl), lambda i, t: (i, 0, t)), row_spec, head_spec, head_spec],
        input_output_aliases={3: 3, 4: 4},
        out_shape=[jax.ShapeDtypeStruct((b, l, GROUP_W), BF16), jax.ShapeDtypeStruct((b, GROUP_W, l), BF16),
                   jax.ShapeDtypeStruct((b, l, GROUP_W), BF16), head_shape, head_shape],
        compiler_params=_cparams("arbitrary", "arbitrary"),
        name="sb_prep",
    )(u_sb, qg, kg, k_all, v_all)


def _sb_kernel(q_ref, kt_ref, v_ref, o_ref, later_scr, acc_scr, *, q_off, qb, kb):
    gw = GROUP_W
    q0 = q_off + pl.program_id(1) * qb
    nk = (q0 + qb - 1) // kb + 1
    head_of_lane = lax.broadcasted_iota(jnp.int32, (qb, gw), 1) // SB_HEAD_DIM
    q = q_ref[0].astype(F32)
    qh = [jnp.where(head_of_lane == h, q, 0.0).astype(BF16) for h in range(SB_HEADS)]
    head_of_vlane = lax.broadcasted_iota(jnp.int32, (kb, gw), 1) // SB_HEAD_DIM
    vmask = [(head_of_vlane == h).astype(BF16) for h in range(SB_HEADS)]
    r = lax.broadcasted_iota(jnp.int32, (kb, kb), 0)
    c = lax.broadcasted_iota(jnp.int32, (kb, kb), 1)
    suffix = (r >= c).astype(BF16)
    qpos = q0 + lax.broadcasted_iota(jnp.int32, (qb, kb), 0)
    kcol = lax.broadcasted_iota(jnp.int32, (qb, kb), 1)
    later_scr[...] = jnp.zeros_like(later_scr)
    acc_scr[...] = jnp.zeros_like(acc_scr)
    sign = jnp.uint32(0x80000000)

    def block(j):
        return pl.ds(pl.multiple_of(j * kb, kb), kb)

    def scores(js, masked=False):
        zs = [_dot(qh[h], kt_ref[0, :, block(j)]) for j in js for h in range(SB_HEADS)]
        if masked:
            zs = [jnp.where(js[i // SB_HEADS] * kb + kcol < qpos, z, SB_NEG) for i, z in enumerate(zs)]
        return zs

    def weights(zs):
        css = []
        for z in zs:
            neg_abs = lax.bitcast_convert_type(lax.bitcast_convert_type(z, jnp.uint32) | sign, F32)
            sp = jnp.maximum(z, 0.0) + jnp.log(1.0 + jnp.exp2(neg_abs)) * LOG2_E
            css.append(_dot(sp.astype(BF16), suffix))
        laters = [later_scr[h] for h in range(SB_HEADS)]
        parts = []
        for i, (z, cs) in enumerate(zip(zs, css)):
            h = i % SB_HEADS
            wide = jnp.concatenate([laters[h]] * (kb // V7X_LANES), axis=1)
            parts.append(jnp.exp2(z - cs - wide).astype(BF16))
            laters[h] = laters[h] + jnp.broadcast_to(cs[:, 0:1], laters[h].shape)
        for h in range(SB_HEADS):
            later_scr[h] = laters[h]
        return jnp.concatenate(parts, axis=1)

    def values(js):
        return jnp.concatenate([v_ref[0, block(j), :] * vmask[h] for j in js for h in range(SB_HEADS)], axis=0)

    def direct(js, masked):
        acc_scr[...] += _dot(weights(scores(js, masked)), values(js))

    direct([nk - 1], True)
    rest = nk - 1
    odd = rest % SB_UNROLL

    @pl.when(odd == 1)
    def _():
        direct([nk - 2], False)

    j0 = nk - 2 - odd

    def body(i, carry):
        direct([j0 - SB_UNROLL * i - s for s in range(SB_UNROLL)], False)
        return carry

    lax.fori_loop(0, rest // SB_UNROLL, body, 0)
    o_ref[0] = acc_scr[...].astype(o_ref.dtype)


def _sb_attention(q, kt_all, v_all, q_off):
    b, lq, gw = q.shape
    lk = v_all.shape[1]
    qb = min(SB_BLOCK, lq)
    kb = SB_BLOCK
    assert lq % qb == 0 and lk % kb == 0 and q_off % kb == 0 and kb % qb == 0 and SB_UNROLL == 2
    assert lk >= q_off + lq
    return pl.pallas_call(
        functools.partial(_sb_kernel, q_off=q_off, qb=qb, kb=kb),
        grid=(b, lq // qb),
        in_specs=[pl.BlockSpec((1, qb, gw), lambda i, t: (i, t, 0)),
                  pl.BlockSpec((1, gw, lk), lambda i, t: (i, 0, 0)),
                  pl.BlockSpec((1, lk, gw), lambda i, t: (i, 0, 0))],
        out_specs=pl.BlockSpec((1, qb, gw), lambda i, t: (i, t, 0)),
        out_shape=jax.ShapeDtypeStruct((b, lq, gw), BF16),
        scratch_shapes=[pltpu.VMEM((SB_HEADS, qb, V7X_LANES), F32), pltpu.VMEM((qb, gw), F32)],
        compiler_params=_cparams("arbitrary", "arbitrary"),
        name="sb_attention",
    )(q, kt_all, v_all)


def _sb_mixer(u_sb, k_prev, v_prev, q_norm, k_norm, k_all, v_all, layer):
    q, kt, vb, k_all, v_all = _sb_prep(u_sb, q_norm, k_norm, k_all, v_all, layer)
    b, heads, past, d = k_prev.shape
    lq = u_sb.shape[1]
    pad = (-(past + lq)) % SB_BLOCK
    if past or pad:
        kt_prev = jnp.swapaxes(k_prev, 2, 3).reshape(b, heads * d, past).astype(BF16)
        v_prev_rows = jnp.swapaxes(v_prev, 1, 2).reshape(b, past, heads * d).astype(BF16)
        kt = jnp.pad(jnp.concatenate([kt_prev, kt], axis=2), ((0, 0), (0, 0), (0, pad)))
        vb = jnp.pad(jnp.concatenate([v_prev_rows, vb], axis=1), ((0, 0), (0, pad), (0, 0)))
    return _sb_attention(q, kt, vb, past), k_all, v_all


def _out_ffn_kernel(x_ref, yc_ref, yd_ref, yp_ref, ys_ref, mod_ref, g_ref, wo_ref, wg_ref, wu_ref, wd_ref, o_ref):
    tb, tl, d = x_ref.shape
    m = tb * tl
    gw = GROUP_W
    ycat = jnp.concatenate([r[...].reshape(m, gw) for r in (yc_ref, yd_ref, yp_ref, ys_ref)], axis=1)
    mix = _dot(ycat, wo_ref[...])
    x1 = x_ref[...] + mod_ref[:, 2:3, :] * mix.reshape(tb, tl, d)
    y = x1 * lax.rsqrt(jnp.mean(x1 * x1, axis=-1, keepdims=True) + EPS) * g_ref[...]
    h2 = (y * (1.0 + mod_ref[:, 4:5, :]) + mod_ref[:, 3:4, :]).reshape(m, d).astype(BF16)
    ff = D_FF // FF_SPLIT
    acc = jnp.zeros((m, d), F32)
    for c in range(FF_SPLIT):
        gate = _dot(h2, wg_ref[:, c * ff:(c + 1) * ff])
        up = _dot(h2, wu_ref[:, c * ff:(c + 1) * ff])
        act = (gate * _sigmoid(gate) * up).astype(BF16)
        acc = acc + _dot(act, wd_ref[c * ff:(c + 1) * ff, :])
    o_ref[...] = x1 + mod_ref[:, 5:6, :] * acc.reshape(tb, tl, d)


def _out_ffn(x, y_conv, y_dn, y_pool, y_sb, mod, norm_g, w_out, w_gate, w_up, w_down, layer):
    b, l, d = x.shape
    tb, tl = _row_tiles(b, l)
    assert (D_FF // FF_SPLIT) % V7X_LANES == 0
    row_spec = lambda n: pl.BlockSpec((tb, tl, n), lambda i, t: (i, t, 0))
    return pl.pallas_call(
        _out_ffn_kernel,
        grid=(b // tb, l // tl),
        in_specs=[row_spec(d), row_spec(GROUP_W), row_spec(GROUP_W), row_spec(GROUP_W), row_spec(GROUP_W),
                  pl.BlockSpec((tb, N_MOD, d), lambda i, t: (i, 0, 0)),
                  _const_spec((1, d)),
                  _const_spec((d, d), layer), _const_spec((d, D_FF), layer),
                  _const_spec((d, D_FF), layer), _const_spec((D_FF, d), layer)],
        out_specs=row_spec(d),
        out_shape=jax.ShapeDtypeStruct((b, l, d), F32),
        compiler_params=_cparams("arbitrary", "arbitrary"),
        name="out_ffn",
    )(x, y_conv, y_dn, y_pool, y_sb, mod, norm_g, w_out, w_gate, w_up, w_down)


def _pack_w_in(w_in):
    c0 = 2 * GROUP_W + 3 * GROUP_W + GROUP_W
    ab = w_in[..., c0:c0 + 2 * DN_HEADS]
    ab = jnp.pad(ab, ((0, 0), (0, 0), (0, AB_W - 2 * DN_HEADS)))
    return jnp.concatenate([w_in[..., :c0], ab, w_in[..., c0 + 2 * DN_HEADS:]], axis=-1).astype(BF16)


def _pool_block_diag(pool_w):
    g, n, _ = pool_w.shape
    out = jnp.zeros((g * n, g * n), pool_w.dtype)
    for i in range(g):
        out = out.at[i * n:(i + 1) * n, i * n:(i + 1) * n].set(pool_w[i])
    return out.astype(BF16)


def _trunk_layer(x, mod, pos0, conv_prev, dn_s0, dn_conv_prev, pool_prev, k_prev, v_prev, k_all, v_all, p, layer):
    u_conv, u_qkv, u_gate, u_ab, u_pool, u_sb = _in_proj(x, mod, p["norm_mix"], p["w_in"], layer)
    y_conv, conv_new = _conv_mixer(u_conv, conv_prev, p["conv_dw_w"], p["conv_dw_b"], p["conv_ln_g"], p["conv_ln_b"])
    y_dn, s_new, dn_conv_new = _dn_mixer(u_qkv, u_gate, u_ab, dn_conv_prev, dn_s0, p["dn_conv_w"],
                                         p["dn_a_log"], p["dn_dt_bias"], p["dn_norm_g"])
    y_pool, pool_new = _pool_mixer(u_pool, pool_prev, p["pool_w"], p["pool_scale"], pos0)
    y_sb, k_all, v_all = _sb_mixer(u_sb, k_prev, v_prev, p["sb_q_norm"], p["sb_k_norm"], k_all, v_all, layer)
    x = _out_ffn(x, y_conv, y_dn, y_pool, y_sb, mod, p["norm_ffn"], p["w_out"], p["ffn_w_gate"],
                 p["ffn_w_up"], p["ffn_w_down"], layer)
    return x, conv_new, s_new, dn_conv_new, pool_new, k_all, v_all


def kernel(x_prompt, x_sample, c_prompt, c_sample, cache_conv, state_dn, cache_dn_conv, cache_pool, cache_sb_k, cache_sb_v, w_ada, b_ada, norm_mix, norm_ffn, w_in, w_out, conv_dw_w, conv_dw_b, conv_ln_g, conv_ln_b, dn_conv_w, dn_a_log, dn_dt_bias, dn_norm_g, pool_w, pool_scale, sb_q_norm, sb_k_norm, ffn_w_gate, ffn_w_up, ffn_w_down):
    depth = w_ada.shape[0]
    bp = x_prompt.shape[0]
    bs = x_sample.shape[0]
    past = cache_sb_k.shape[3]
    dt = x_prompt.dtype

    mod_all = _ada_modulation(jnp.concatenate([c_prompt, c_sample], axis=0), w_ada, b_ada)
    mod_all = mod_all.reshape(depth, bp + bs, N_MOD, D_MODEL)
    w_in_p = _pack_w_in(w_in)
    w_out_b, w_gate_b, w_up_b, w_down_b = (w.astype(BF16) for w in (w_out, ffn_w_gate, ffn_w_up, ffn_w_down))

    zeros_p = dict(
        conv=jnp.zeros((bp, CONV_W - 1, GROUP_W), dt),
        s0=jnp.zeros((bp, DN_HEADS, DN_HEAD_DIM, DN_HEAD_DIM), F32),
        dn_conv=jnp.zeros((bp, DN_CONV - 1, 3 * GROUP_W), dt),
        pool=jnp.zeros((bp, POOL_PREFIX, GROUP_W), dt),
        kv=jnp.zeros((bp, SB_HEADS, 0, SB_HEAD_DIM), dt))

    xp, xs = x_prompt, x_sample
    new_p = [[] for _ in range(4)]
    new_s = [[] for _ in range(4)]
    cache_shape = lambda x: (depth, x.shape[0], SB_HEADS, x.shape[1], SB_HEAD_DIM)
    sbk_p, sbv_p = jnp.zeros(cache_shape(xp), F32), jnp.zeros(cache_shape(xp), F32)
    sbk_s, sbv_s = jnp.zeros(cache_shape(xs), F32), jnp.zeros(cache_shape(xs), F32)
    for l in range(depth):
        row = lambda a: a[l].reshape(1, -1)
        p = dict(norm_mix=row(norm_mix), norm_ffn=row(norm_ffn), w_in=w_in_p, w_out=w_out_b,
                 conv_dw_w=conv_dw_w[l], conv_dw_b=row(conv_dw_b), conv_ln_g=row(conv_ln_g), conv_ln_b=row(conv_ln_b),
                 dn_conv_w=dn_conv_w[l], dn_a_log=dn_a_log[l], dn_dt_bias=dn_dt_bias[l], dn_norm_g=dn_norm_g[l],
                 pool_w=_pool_block_diag(pool_w[l]), pool_scale=row(pool_scale),
                 sb_q_norm=row(sb_q_norm), sb_k_norm=row(sb_k_norm),
                 ffn_w_gate=w_gate_b, ffn_w_up=w_up_b, ffn_w_down=w_down_b)
        xp, *sp, sbk_p, sbv_p = _trunk_layer(xp, mod_all[l, :bp], 0, zeros_p["conv"], zeros_p["s0"], zeros_p["dn_conv"],
                                             zeros_p["pool"], zeros_p["kv"], zeros_p["kv"], sbk_p, sbv_p, p, l)
        xs, *ss, sbk_s, sbv_s = _trunk_layer(xs, mod_all[l, bp:], past, cache_conv[l], state_dn[l], cache_dn_conv[l],
                                             cache_pool[l], cache_sb_k[l], cache_sb_v[l], sbk_s, sbv_s, p, l)
        for i in range(4):
            new_p[i].append(sp[i])
            new_s[i].append(ss[i])
    conv_p, dn_p, dnconv_p, pool_p = [jnp.stack(a) for a in new_p]
    conv_s, dn_s, dnconv_s, pool_s = [jnp.stack(a) for a in new_s]
    return (xp, xs, conv_p, conv_s, dn_p, dn_s, dnconv_p, dnconv_s, pool_p, pool_s, sbk_p, sbk_s, sbv_p, sbv_s)
```

```python
import functools

import jax
import jax.numpy as jnp
from jax import lax
from jax.experimental import pallas as pl
from jax.experimental.pallas import tpu as pltpu

D_MODEL = 1024
GROUP_W = 256
CONV_W = 31
DN_HEADS = 4
DN_HEAD_DIM = 64
DN_CONV = 4
CHUNK = 64
POOL_PREFIX = 15
SB_HEADS = 4
SB_HEAD_DIM = 64
SB_NEG = -1e30
D_FF = 2816
N_MOD = 6
EPS = 1e-6

V7X_LANES = 128
V7X_SUBLANES = 8
V7X_MXU_DIM = 256
V7X_VMEM_BYTES = 64 * 1024 * 1024
VMEM_LIMIT = V7X_VMEM_BYTES * 7 // 8

ROW_TILE = 512
FF_SPLIT = 2
CONV_PAD = 32
DN_PAD = 8
POOL_PAD = 16
AB_W = V7X_LANES
DN_UNROLL = 4
SB_BLOCK = V7X_MXU_DIM
SB_UNROLL = 2
LOG2_E = 1.4426950408889634
SQ_PARTS = 1
IN_WIDTHS = (2 * GROUP_W, 3 * GROUP_W, GROUP_W, AB_W, GROUP_W, 3 * GROUP_W)

F32 = jnp.float32
BF16 = jnp.bfloat16


def _cparams(*sem):
    return pltpu.CompilerParams(dimension_semantics=sem, vmem_limit_bytes=VMEM_LIMIT)


def _dot(a, b):
    return jnp.dot(a, b, preferred_element_type=F32)


def _dot_nt(a, b):
    return lax.dot_general(a, b, (((1,), (1,)), ((), ())), preferred_element_type=F32)


def _dot_tn(a, b):
    return lax.dot_general(a, b, (((0,), (0,)), ((), ())), preferred_element_type=F32)


def _split(x, parts):
    out = []
    for _ in range(parts):
        p = x.astype(BF16)
        out.append(p)
        x = x - p.astype(F32)
    return out


def _dot_sel_rhs(a, sel, parts):
    pieces = _split(a, parts)
    out = _dot(pieces[0], sel)
    for p in pieces[1:]:
        out = out + _dot(p, sel)
    return out


def _dot_sel_lhs(sel, b, parts):
    pieces = _split(b, parts)
    out = _dot(sel, pieces[0])
    for p in pieces[1:]:
        out = out + _dot(sel, p)
    return out


def _sigmoid(x):
    return 1.0 / (1.0 + jnp.exp(-x))


def _softplus(x):
    return jnp.maximum(x, 0.0) + jnp.log1p(jnp.exp(-jnp.abs(x)))


def _row_tiles(batch, length):
    if length >= ROW_TILE:
        assert length % ROW_TILE == 0
        return 1, ROW_TILE
    tb = max(1, min(batch, ROW_TILE // length))
    while batch % tb:
        tb -= 1
    return tb, length


def _const_spec(shape, layer=None):
    nd = len(shape)
    if layer is None:
        return pl.BlockSpec(shape, lambda *_: (0,) * nd, pipeline_mode=pl.Buffered(1))
    return pl.BlockSpec((None,) + shape, lambda *_: (layer,) + (0,) * nd, pipeline_mode=pl.Buffered(1))


def _head_ones(n, dtype):
    r = lax.broadcasted_iota(jnp.int32, (n, n), 0) // DN_HEAD_DIM
    c = lax.broadcasted_iota(jnp.int32, (n, n), 1) // DN_HEAD_DIM
    return (r == c).astype(dtype)


def _ada_kernel(c_ref, w_ref, b_ref, o_ref):
    c = c_ref[...]
    a = (c * _sigmoid(c)).astype(BF16)
    o_ref[0] = _dot(a, w_ref[0].astype(BF16)) + b_ref[0]


def _ada_modulation(c_all, w_ada, b_ada):
    depth, d, n = w_ada.shape
    nb = c_all.shape[0]
    tn = 1536
    assert n % tn == 0
    return pl.pallas_call(
        _ada_kernel,
        grid=(depth, n // tn),
        in_specs=[pl.BlockSpec((nb, d), lambda l, j: (0, 0)),
                  pl.BlockSpec((1, d, tn), lambda l, j: (l, 0, j)),
                  pl.BlockSpec((1, 1, tn), lambda l, j: (l, 0, j))],
        out_specs=pl.BlockSpec((1, nb, tn), lambda l, j: (l, 0, j)),
        out_shape=jax.ShapeDtypeStruct((depth, nb, n), F32),
        compiler_params=_cparams("arbitrary", "arbitrary"),
        name="ada_modulation",
    )(c_all, w_ada, b_ada.reshape(depth, 1, n))


def _conv_front(u, prev_ref, w_ref, b_ref, g_ref, beta_ref, y_ref, tail_ref, full_scr, shift_scr, bi, first_step):
    tl = u.shape[0]

    @pl.when(first_step)
    def _():
        full_scr[bi, 0:CONV_PAD, :] = prev_ref[bi]

    full_scr[bi, CONV_PAD:CONV_PAD + tl, :] = u[:, :GROUP_W] * _sigmoid(u[:, GROUP_W:])
    sub = min(tl, 64)
    first = CONV_PAD - (CONV_W - 1)
    span = shift_scr.shape[1]
    for phase in range(1, V7X_SUBLANES):
        shift_scr[phase - 1] = full_scr[bi, phase:phase + span, :]
    for r in range(0, tl, sub):
        acc = jnp.zeros((sub, GROUP_W), F32)
        for k in range(CONV_W):
            phase, lo = (first + k) % V7X_SUBLANES, (first + k) // V7X_SUBLANES * V7X_SUBLANES + r
            rows = full_scr[bi, lo:lo + sub, :] if phase == 0 else shift_scr[phase - 1, lo:lo + sub, :]
            acc = acc + w_ref[k:k + 1, :] * rows
        y = acc + b_ref[...]
        mu = jnp.mean(y, axis=-1, keepdims=True)
        yc = y - mu
        var = jnp.mean(yc * yc, axis=-1, keepdims=True)
        yn = yc * lax.rsqrt(var + EPS) * g_ref[...] + beta_ref[...]
        y_ref[bi, r:r + sub, :] = (yn * _sigmoid(yn)).astype(y_ref.dtype)
    tail = full_scr[bi, tl:tl + CONV_PAD, :]
    tail_ref[bi] = tail
    full_scr[bi, 0:CONV_PAD, :] = tail


def _pool_front(cur, prev_ref, w_ref, scale_ref, y_ref, tail_ref, full_scr, bi, first_step, pos):
    tl = cur.shape[0]

    @pl.when(first_step)
    def _():
        full_scr[bi, 0:POOL_PAD, :] = prev_ref[bi]

    full_scr[bi, POOL_PAD:POOL_PAD + tl, :] = cur
    lane = lax.broadcasted_iota(jnp.int32, (tl, GROUP_W), 1)
    row = lax.broadcasted_iota(jnp.int32, (tl, GROUP_W), 0)
    group = lane // (GROUP_W // 4)
    acc = cur
    win = None
    for i in range(1, 16):
        acc = acc + full_scr[bi, POOL_PAD - i:POOL_PAD - i + tl, :]
        if i in (1, 3, 7, 15):
            gi = (1, 3, 7, 15).index(i)
            win = acc if win is None else jnp.where(group >= gi, acc, win)
    width = jnp.left_shift(2, group)
    cnt = jnp.minimum(pos + row + 1, width).astype(F32)
    pooled = win / cnt - cur
    y = _dot(pooled.astype(BF16), w_ref[...]) * scale_ref[...]
    y_ref[bi] = y.astype(y_ref.dtype)
    tail = full_scr[bi, tl:tl + POOL_PAD, :]
    tail_ref[bi] = tail
    full_scr[bi, 0:POOL_PAD, :] = tail


def _dn_front(u, prev_ref, cw_ref, tail_ref, full_scr, bi, first_step):
    tl = u.shape[0]

    @pl.when(first_step)
    def _():
        full_scr[bi, 0:DN_PAD, :] = prev_ref[bi]

    full_scr[bi, DN_PAD:DN_PAD + tl, :] = u
    first = DN_PAD - (DN_CONV - 1)
    acc = cw_ref[0:1, :] * full_scr[bi, first:first + tl, :]
    for k in range(1, DN_CONV):
        acc = acc + cw_ref[k:k + 1, :] * full_scr[bi, first + k:first + k + tl, :]
    tail = full_scr[bi, tl:tl + DN_PAD, :]
    tail_ref[bi] = tail
    full_scr[bi, 0:DN_PAD, :] = tail
    return acc * _sigmoid(acc)


def _front_kernel(x_ref, mod_ref, ng_ref, w_ref,
                  cprev_ref, cw_ref, cb_ref, cg_ref, cbeta_ref,
                  pprev_ref, pw_ref, pscale_ref,
                  dprev_ref, dcw_ref, alog_ref, dtb_ref,
                  qg_ref, kg_ref, k_all_ref, v_all_ref,
                  yc_ref, ctail_ref, yp_ref, ptail_ref,
                  dq_ref, dk_ref, dv_ref, dg_ref, db_ref, gate_ref, dtail_ref,
                  sq_ref, skt_ref, svb_ref, sk_ref, sv_ref,
                  cfull, cshift, pfull, dfull, *, pos0):
    del k_all_ref, v_all_ref
    tb, tl, d = x_ref.shape
    m = tb * tl
    gw = GROUP_W
    t = pl.program_id(1)
    first_step = t == 0
    x = x_ref[...]
    y = x * lax.rsqrt(jnp.mean(x * x, axis=-1, keepdims=True) + EPS) * ng_ref[...]
    hb = (y * (1.0 + mod_ref[:, 1:2, :]) + mod_ref[:, 0:1, :]).reshape(m, d).astype(BF16)
    starts = [sum(IN_WIDTHS[:i]) for i in range(len(IN_WIDTHS))]
    proj = lambda i: _dot(hb, w_ref[:, starts[i]:starts[i] + IN_WIDTHS[i]])
    seqs = [slice(bi * tl, (bi + 1) * tl) for bi in range(tb)]

    u_conv = proj(0)
    for bi, rows in enumerate(seqs):
        _conv_front(u_conv[rows], cprev_ref, cw_ref, cb_ref, cg_ref, cbeta_ref, yc_ref, ctail_ref, cfull, cshift,
                    bi, first_step)

    u_qkv = proj(1)
    qkv = jnp.concatenate([_dn_front(u_qkv[rows], dprev_ref, dcw_ref, dtail_ref, dfull, bi, first_step)
                           for bi, rows in enumerate(seqs)], axis=0)
    ones_bd = _head_ones(gw, BF16)
    q = qkv[:, 0:gw]
    k = qkv[:, gw:2 * gw]
    dq_ref[...] = (q * lax.rsqrt(_dot_sel_rhs(q * q, ones_bd, SQ_PARTS) + EPS) * (DN_HEAD_DIM ** -0.5)).reshape(tb, tl, gw)
    dk_ref[...] = (k * lax.rsqrt(_dot_sel_rhs(k * k, ones_bd, SQ_PARTS) + EPS)).reshape(tb, tl, gw)
    dv_ref[...] = qkv[:, 2 * gw:3 * gw].reshape(tb, tl, gw)

    gate_ref[...] = proj(2).reshape(tb, tl, gw)

    ab = proj(3)
    g_all = -jnp.exp(alog_ref[...]) * _softplus(ab + dtb_ref[...])
    lane = lax.broadcasted_iota(jnp.int32, ab.shape, 1)
    gb = jnp.where(lane < DN_HEADS, g_all, _sigmoid(ab))
    er = lax.broadcasted_iota(jnp.int32, (AB_W, 2 * gw), 0)
    ec = lax.broadcasted_iota(jnp.int32, (AB_W, 2 * gw), 1)
    expand = (er == jnp.where(ec < gw, ec // DN_HEAD_DIM, DN_HEADS + (ec - gw) // DN_HEAD_DIM)).astype(BF16)
    gbx = _dot_sel_rhs(gb, expand, 3)
    dg_ref[...] = gbx[:, 0:gw].reshape(tb, tl, gw)
    db_ref[...] = gbx[:, gw:2 * gw].reshape(tb, tl, gw)

    u_pool = proj(4)
    for bi, rows in enumerate(seqs):
        _pool_front(u_pool[rows], pprev_ref, pw_ref, pscale_ref, yp_ref, ptail_ref, pfull, bi, first_step,
                    pos0 + t * tl)

    u_sb = proj(5)
    sq = u_sb[:, 0:gw]
    sk = u_sb[:, gw:2 * gw]
    sv = u_sb[:, 2 * gw:3 * gw]
    inv_d = 1.0 / SB_HEAD_DIM
    qn = sq * lax.rsqrt(_dot_sel_rhs(sq * sq, ones_bd, SQ_PARTS) * inv_d + EPS) * qg_ref[...]
    kn = sk * lax.rsqrt(_dot_sel_rhs(sk * sk, ones_bd, SQ_PARTS) * inv_d + EPS) * kg_ref[...]
    sq_ref[...] = (qn * (SB_HEAD_DIM ** -0.5 * LOG2_E)).astype(BF16).reshape(tb, tl, gw)
    svb_ref[...] = sv.astype(BF16).reshape(tb, tl, gw)
    for bi, rows in enumerate(seqs):
        skt_ref[bi] = kn[rows].T.astype(BF16)
        for h in range(SB_HEADS):
            hs = slice(h * SB_HEAD_DIM, (h + 1) * SB_HEAD_DIM)
            sk_ref[bi, h] = kn[rows, hs]
            sv_ref[bi, h] = sv[rows, hs]


def _front(x, mod, pos0, conv_prev, dn_conv_prev, pool_prev, k_all, v_all, p, layer):
    b, l, d = x.shape
    tb, tl = _row_tiles(b, l)
    gw, gw3 = GROUP_W, 3 * GROUP_W
    cprev = jnp.pad(conv_prev, ((0, 0), (CONV_PAD - (CONV_W - 1), 0), (0, 0)))
    pprev = jnp.pad(pool_prev, ((0, 0), (POOL_PAD - POOL_PREFIX, 0), (0, 0)))
    dprev = jnp.pad(dn_conv_prev, ((0, 0), (DN_PAD - (DN_CONV - 1), 0), (0, 0)))
    alog = jnp.pad(p["dn_a_log"], (0, AB_W - DN_HEADS)).reshape(1, AB_W)
    dtb = jnp.pad(p["dn_dt_bias"], (0, AB_W - DN_HEADS)).reshape(1, AB_W)
    qg = jnp.tile(p["sb_q_norm"], (1, SB_HEADS))
    kg = jnp.tile(p["sb_k_norm"], (1, SB_HEADS))
    assert k_all.shape[1:] == (b, SB_HEADS, l, SB_HEAD_DIM) and v_all.shape == k_all.shape
    rows = lambda n: pl.BlockSpec((tb, tl, n), lambda i, t: (i, t, 0))
    per_seq = lambda r, n: pl.BlockSpec((tb, r, n), lambda i, t: (i, 0, 0))
    cache = pl.BlockSpec((None, tb, SB_HEADS, tl, SB_HEAD_DIM), lambda i, t: (layer, i, 0, t, 0))
    f32_rows = jax.ShapeDtypeStruct((b, l, gw), F32)
    bf_rows = jax.ShapeDtypeStruct((b, l, gw), BF16)
    out = pl.pallas_call(
        functools.partial(_front_kernel, pos0=pos0),
        grid=(b // tb, l // tl),
        in_specs=[rows(d), per_seq(N_MOD, d), _const_spec((1, d)), _const_spec((d, sum(IN_WIDTHS)), layer),
                  per_seq(CONV_PAD, gw), _const_spec((CONV_W, gw)), _const_spec((1, gw)), _const_spec((1, gw)),
                  _const_spec((1, gw)),
                  per_seq(POOL_PAD, gw), _const_spec((gw, gw)), _const_spec((1, gw)),
                  per_seq(DN_PAD, gw3), _const_spec((DN_CONV, gw3)), _const_spec((1, AB_W)), _const_spec((1, AB_W)),
                  _const_spec((1, gw)), _const_spec((1, gw)),
                  pl.BlockSpec(memory_space=pl.ANY), pl.BlockSpec(memory_space=pl.ANY)],
        out_specs=[rows(gw), per_seq(CONV_PAD, gw), rows(gw), per_seq(POOL_PAD, gw),
                   rows(gw), rows(gw), rows(gw), rows(gw), rows(gw), rows(gw), per_seq(DN_PAD, gw3),
                   rows(gw), pl.BlockSpec((tb, gw, tl), lambda i, t: (i, 0, t)), rows(gw), cache, cache],
        out_shape=[bf_rows, jax.ShapeDtypeStruct((b, CONV_PAD, gw), F32),
                   bf_rows, jax.ShapeDtypeStruct((b, POOL_PAD, gw), F32),
                   f32_rows, f32_rows, f32_rows, f32_rows, f32_rows, f32_rows,
                   jax.ShapeDtypeStruct((b, DN_PAD, gw3), F32),
                   bf_rows, jax.ShapeDtypeStruct((b, gw, l), BF16), bf_rows,
                   jax.ShapeDtypeStruct(k_all.shape, F32), jax.ShapeDtypeStruct(v_all.shape, F32)],
        input_output_aliases={18: 14, 19: 15},
        scratch_shapes=[pltpu.VMEM((tb, CONV_PAD + tl, gw), F32),
                        pltpu.VMEM((V7X_SUBLANES - 1, CONV_PAD - V7X_SUBLANES + tl, gw), F32),
                        pltpu.VMEM((tb, POOL_PAD + tl, gw), F32),
                        pltpu.VMEM((tb, DN_PAD + tl, gw3), F32)],
        compiler_params=_cparams("arbitrary", "arbitrary"),
        name="front",
    )(x, mod, p["norm_mix"], p["w_in"],
      cprev, p["conv_dw_w"], p["conv_dw_b"], p["conv_ln_g"], p["conv_ln_b"],
      pprev, p["pool_w"], p["pool_scale"],
      dprev, p["dn_conv_w"], alog, dtb, qg, kg, k_all, v_all)
    (y_conv, ctail, y_pool, ptail, dq, dk, dv, dg, db, gate, dtail, sq, skt, svb, k_all, v_all) = out
    tails = (ctail[:, CONV_PAD - (CONV_W - 1):], dtail[:, DN_PAD - (DN_CONV - 1):], ptail[:, POOL_PAD - POOL_PREFIX:])
    return y_conv, y_pool, (dq, dk, dv, dg, db, gate), (sq, skt, svb), tails, k_all, v_all


def _unit_lower_inverses(lmats):
    c = lmats[0].shape[0]
    eye = (lax.broadcasted_iota(jnp.int32, (c, c), 0) == lax.broadcasted_iota(jnp.int32, (c, c), 1)).astype(F32)
    xs = [eye - m for m in lmats]
    ps = []
    for m in lmats:
        mh, ml = _split(m, 2)
        r1 = _dot(jnp.concatenate([mh, ml], axis=0), mh)
        ps.append(r1[0:c] + r1[c:2 * c] + _dot(mh, ml))
    span = 2
    while True:
        span *= 2
        last = span >= c
        nxt_x, nxt_p = [], []
        for x, p in zip(xs, ps):
            xh, xl = _split(x, 2)
            ph, pl_ = _split(p, 2)
            if last:
                r1 = _dot(jnp.concatenate([xh, xl], axis=0), ph)
                nxt_x.append(x + (r1[0:c] + r1[c:2 * c] + _dot(xh, pl_)))
            else:
                r1 = _dot(jnp.concatenate([xh, xl, ph, pl_], axis=0), ph)
                r2 = _dot(jnp.concatenate([xh, ph], axis=0), pl_)
                nxt_x.append(x + (r1[0:c] + r1[c:2 * c] + r2[0:c]))
                nxt_p.append(r1[2 * c:3 * c] + r1[3 * c:4 * c] + r2[c:2 * c])
        xs, ps = nxt_x, nxt_p
        if last:
            return xs


def _dn_kernel(q_ref, k_ref, v_ref, g_ref, b_ref, gate_ref, s0_ref, ng_ref, y_ref, s_ref, o_scr):
    tl = q_ref.shape[1]
    gw = GROUP_W
    hd = DN_HEAD_DIM

    @pl.when(pl.program_id(1) == 0)
    def _():
        s_ref[0] = s0_ref[0]

    ones_bd = _head_ones(gw, BF16)
    ci = lax.broadcasted_iota(jnp.int32, (CHUNK, CHUNK), 0)
    cj = lax.broadcasted_iota(jnp.int32, (CHUNK, CHUNK), 1)
    causal = ci >= cj
    strict = ci > cj
    tri = causal.astype(BF16)
    n_sub = min(DN_UNROLL, tl // CHUNK)
    heads = [slice(h * hd, (h + 1) * hd) for h in range(DN_HEADS)]

    def chunk_group(c, carry):
        pre = []
        lmats, attns = [], []
        for s in range(n_sub):
            rows = pl.ds(pl.multiple_of((c * n_sub + s) * CHUNK, CHUNK), CHUNK)
            qc, kc, vc, bc = q_ref[0, rows, :], k_ref[0, rows, :], v_ref[0, rows, :], b_ref[0, rows, :]
            gcb = _dot_sel_lhs(tri, g_ref[0, rows, :], 3)
            eg = jnp.exp(gcb)
            glast = gcb[CHUNK - 1:CHUNK, :]
            kb = kc * bc
            pre.append(dict(rows=rows, qc=qc, kc=kc, kb=kb, qd=qc * eg, ku=kc * jnp.exp(glast - gcb),
                            wr=kb * eg, vb=vc * bc, eglast=jnp.exp(glast)))
            for hs in heads:
                gh = gcb[:, hs]
                diff = gh - gh.T
                decay = jnp.where(causal, jnp.exp(jnp.where(causal, diff, 0.0)), 0.0)
                both = _dot_nt(jnp.concatenate([kb[:, hs], qc[:, hs]], axis=0).astype(BF16), kc[:, hs].astype(BF16))
                lmats.append(jnp.where(strict, both[0:CHUNK] * decay, 0.0))
                attns.append((both[CHUNK:2 * CHUNK] * decay).astype(BF16))
        tinvs = _unit_lower_inverses(lmats)
        sols = []
        for s in range(n_sub):
            p = pre[s]
            for h, hs in enumerate(heads):
                th, tlo = _split(tinvs[s * DN_HEADS + h], 2)
                rh, rlo = _split(jnp.concatenate([p["vb"][:, hs], p["wr"][:, hs]], axis=1), 2)
                r1 = _dot(jnp.concatenate([th, tlo], axis=0), rh)
                sols.append(r1[0:CHUNK] + r1[CHUNK:2 * CHUNK] + _dot(th, rlo))
        states = [s_ref[0, h] for h in range(DN_HEADS)]
        for s in range(n_sub):
            p = pre[s]
            outs = []
            for h, hs in enumerate(heads):
                i = s * DN_HEADS + h
                ws = _dot(jnp.concatenate([sols[i][:, hd:], p["qd"][:, hs]], axis=0).astype(BF16), states[h].astype(BF16))
                v_new = (sols[i][:, :hd] - ws[0:CHUNK]).astype(BF16)
                outs.append(ws[CHUNK:2 * CHUNK] + _dot(attns[i], v_new))
                states[h] = states[h] * p["eglast"][:, hs] + _dot_tn(p["ku"][:, hs].astype(BF16), v_new)
            o_scr[p["rows"], :] = jnp.concatenate(outs, axis=1)
        for h in range(DN_HEADS):
            s_ref[0, h] = states[h]
        return carry

    lax.fori_loop(0, tl // (CHUNK * n_sub), chunk_group, 0)

    o = o_scr[...]
    ms = _dot_sel_rhs(o * o, ones_bd, SQ_PARTS) * (1.0 / hd)
    gate = gate_ref[0]
    y_ref[0] = (o * lax.rsqrt(ms + EPS) * ng_ref[...] * (gate * _sigmoid(gate))).astype(y_ref.dtype)


def _dn_mixer(q, k, v, g, beta, gate, s0, norm_g):
    b, l, _ = q.shape
    assert l % CHUNK == 0
    tl = min(l, ROW_TILE)
    assert (tl // CHUNK) % min(DN_UNROLL, tl // CHUNK) == 0
    ng_p = jnp.tile(norm_g, DN_HEADS).reshape(1, GROUP_W)
    rows = pl.BlockSpec((1, tl, GROUP_W), lambda i, t: (i, t, 0))
    state_spec = pl.BlockSpec((1, DN_HEADS, DN_HEAD_DIM, DN_HEAD_DIM), lambda i, t: (i, 0, 0, 0))
    return pl.pallas_call(
        _dn_kernel,
        grid=(b, l // tl),
        in_specs=[rows] * 6 + [state_spec, _const_spec((1, GROUP_W))],
        out_specs=[rows, state_spec],
        out_shape=[jax.ShapeDtypeStruct((b, l, GROUP_W), BF16),
                   jax.ShapeDtypeStruct((b, DN_HEADS, DN_HEAD_DIM, DN_HEAD_DIM), F32)],
        scratch_shapes=[pltpu.VMEM((tl, GROUP_W), F32)],
        compiler_params=_cparams("arbitrary", "arbitrary"),
        name="deltanet_mixer",
    )(q, k, v, g, beta, gate, s0, ng_p)


def _sb_kernel(q_ref, kt_ref, v_ref, o_ref, later_scr, acc_scr, *, q_off, qb, kb):
    gw = GROUP_W
    q0 = q_off + pl.program_id(1) * qb
    nk = (q0 + qb - 1) // kb + 1
    head_of_lane = lax.broadcasted_iota(jnp.int32, (qb, gw), 1) // SB_HEAD_DIM
    q = q_ref[0].astype(F32)
    qh = [jnp.where(head_of_lane == h, q, 0.0).astype(BF16) for h in range(SB_HEADS)]
    head_of_vlane = lax.broadcasted_iota(jnp.int32, (kb, gw), 1) // SB_HEAD_DIM
    vmask = [(head_of_vlane == h).astype(BF16) for h in range(SB_HEADS)]
    r = lax.broadcasted_iota(jnp.int32, (kb, kb), 0)
    c = lax.broadcasted_iota(jnp.int32, (kb, kb), 1)
    suffix = (r >= c).astype(BF16)
    qpos = q0 + lax.broadcasted_iota(jnp.int32, (qb, kb), 0)
    kcol = lax.broadcasted_iota(jnp.int32, (qb, kb), 1)
    later_scr[...] = jnp.zeros_like(later_scr)
    acc_scr[...] = jnp.zeros_like(acc_scr)
    sign = jnp.uint32(0x80000000)

    def block(j):
        return pl.ds(pl.multiple_of(j * kb, kb), kb)

    def scores(js, masked=False):
        zs = [_dot(qh[h], kt_ref[0, :, block(j)]) for j in js for h in range(SB_HEADS)]
        if masked:
            zs = [jnp.where(js[i // SB_HEADS] * kb + kcol < qpos, z, SB_NEG) for i, z in enumerate(zs)]
        return zs

    def weights(zs):
        css = []
        for z in zs:
            neg_abs = lax.bitcast_convert_type(lax.bitcast_convert_type(z, jnp.uint32) | sign, F32)
            sp = jnp.maximum(z, 0.0) + jnp.log(1.0 + jnp.exp2(neg_abs)) * LOG2_E
            css.append(_dot(sp.astype(BF16), suffix))
        laters = [later_scr[h] for h in range(SB_HEADS)]
        parts = []
        for i, (z, cs) in enumerate(zip(zs, css)):
            h = i % SB_HEADS
            wide = jnp.concatenate([laters[h]] * (kb // V7X_LANES), axis=1)
            parts.append(jnp.exp2(z - cs - wide).astype(BF16))
            laters[h] = laters[h] + jnp.broadcast_to(cs[:, 0:1], laters[h].shape)
        for h in range(SB_HEADS):
            later_scr[h] = laters[h]
        return jnp.concatenate(parts, axis=1)

    def values(js):
        return jnp.concatenate([v_ref[0, block(j), :] * vmask[h] for j in js for h in range(SB_HEADS)], axis=0)

    def direct(js, masked):
        acc_scr[...] += _dot(weights(scores(js, masked)), values(js))

    direct([nk - 1], True)
    rest = nk - 1
    odd = rest % SB_UNROLL

    @pl.when(odd == 1)
    def _():
        direct([nk - 2], False)

    j0 = nk - 2 - odd

    def body(i, carry):
        direct([j0 - SB_UNROLL * i - s for s in range(SB_UNROLL)], False)
        return carry

    lax.fori_loop(0, rest // SB_UNROLL, body, 0)
    o_ref[0] = acc_scr[...].astype(o_ref.dtype)


def _sb_attention(q, kt_all, v_all, q_off):
    b, lq, gw = q.shape
    lk = v_all.shape[1]
    qb = min(SB_BLOCK, lq)
    kb = SB_BLOCK
    assert lq % qb == 0 and lk % kb == 0 and q_off % kb == 0 and kb % qb == 0 and SB_UNROLL == 2
    assert lk >= q_off + lq
    return pl.pallas_call(
        functools.partial(_sb_kernel, q_off=q_off, qb=qb, kb=kb),
        grid=(b, lq // qb),
        in_specs=[pl.BlockSpec((1, qb, gw), lambda i, t: (i, t, 0)),
                  pl.BlockSpec((1, gw, lk), lambda i, t: (i, 0, 0)),
                  pl.BlockSpec((1, lk, gw), lambda i, t: (i, 0, 0))],
        out_specs=pl.BlockSpec((1, qb, gw), lambda i, t: (i, t, 0)),
        out_shape=jax.ShapeDtypeStruct((b, lq, gw), BF16),
        scratch_shapes=[pltpu.VMEM((SB_HEADS, qb, V7X_LANES), F32), pltpu.VMEM((qb, gw), F32)],
        compiler_params=_cparams("arbitrary", "arbitrary"),
        name="sb_attention",
    )(q, kt_all, v_all)


def _sb_mixer(q, kt, vb, k_prev, v_prev):
    b, heads, past, d = k_prev.shape
    lq = q.shape[1]
    pad = (-(past + lq)) % SB_BLOCK
    if past or pad:
        kt_prev = jnp.swapaxes(k_prev, 2, 3).reshape(b, heads * d, past).astype(BF16)
        v_prev_rows = jnp.swapaxes(v_prev, 1, 2).reshape(b, past, heads * d).astype(BF16)
        kt = jnp.pad(jnp.concatenate([kt_prev, kt], axis=2), ((0, 0), (0, 0), (0, pad)))
        vb = jnp.pad(jnp.concatenate([v_prev_rows, vb], axis=1), ((0, 0), (0, pad), (0, 0)))
    return _sb_attention(q, kt, vb, past)


def _out_ffn_kernel(x_ref, yc_ref, yd_ref, yp_ref, ys_ref, mod_ref, g_ref, wo_ref, wg_ref, wu_ref, wd_ref, o_ref):
    tb, tl, d = x_ref.shape
    m = tb * tl
    gw = GROUP_W
    ycat = jnp.concatenate([r[...].reshape(m, gw) for r in (yc_ref, yd_ref, yp_ref, ys_ref)], axis=1)
    mix = _dot(ycat, wo_ref[...])
    x1 = x_ref[...] + mod_ref[:, 2:3, :] * mix.reshape(tb, tl, d)
    y = x1 * lax.rsqrt(jnp.mean(x1 * x1, axis=-1, keepdims=True) + EPS) * g_ref[...]
    h2 = (y * (1.0 + mod_ref[:, 4:5, :]) + mod_ref[:, 3:4, :]).reshape(m, d).astype(BF16)
    ff = D_FF // FF_SPLIT
    acc = jnp.zeros((m, d), F32)
    for c in range(FF_SPLIT):
        gate = _dot(h2, wg_ref[:, c * ff:(c + 1) * ff])
        up = _dot(h2, wu_ref[:, c * ff:(c + 1) * ff])
        act = (gate * _sigmoid(gate) * up).astype(BF16)
        acc = acc + _dot(act, wd_ref[c * ff:(c + 1) * ff, :])
    o_ref[...] = x1 + mod_ref[:, 5:6, :] * acc.reshape(tb, tl, d)


def _out_ffn(x, y_conv, y_dn, y_pool, y_sb, mod, norm_g, w_out, w_gate, w_up, w_down, layer):
    b, l, d = x.shape
    tb, tl = _row_tiles(b, l)
    assert (D_FF // FF_SPLIT) % V7X_LANES == 0
    row_spec = lambda n: pl.BlockSpec((tb, tl, n), lambda i, t: (i, t, 0))
    return pl.pallas_call(
        _out_ffn_kernel,
        grid=(b // tb, l // tl),
        in_specs=[row_spec(d), row_spec(GROUP_W), row_spec(GROUP_W), row_spec(GROUP_W), row_spec(GROUP_W),
                  pl.BlockSpec((tb, N_MOD, d), lambda i, t: (i, 0, 0)),
                  _const_spec((1, d)),
                  _const_spec((d, d), layer), _const_spec((d, D_FF), layer),
                  _const_spec((d, D_FF), layer), _const_spec((D_FF, d), layer)],
        out_specs=row_spec(d),
        out_shape=jax.ShapeDtypeStruct((b, l, d), F32),
        compiler_params=_cparams("arbitrary", "arbitrary"),
        name="out_ffn",
    )(x, y_conv, y_dn, y_pool, y_sb, mod, norm_g, w_out, w_gate, w_up, w_down)


def _pack_w_in(w_in):
    c0 = 2 * GROUP_W + 3 * GROUP_W + GROUP_W
    ab = w_in[..., c0:c0 + 2 * DN_HEADS]
    ab = jnp.pad(ab, ((0, 0), (0, 0), (0, AB_W - 2 * DN_HEADS)))
    return jnp.concatenate([w_in[..., :c0], ab, w_in[..., c0 + 2 * DN_HEADS:]], axis=-1).astype(BF16)


def _pool_block_diag(pool_w):
    g, n, _ = pool_w.shape
    out = jnp.zeros((g * n, g * n), pool_w.dtype)
    for i in range(g):
        out = out.at[i * n:(i + 1) * n, i * n:(i + 1) * n].set(pool_w[i])
    return out.astype(BF16)


def _trunk_layer(x, mod, pos0, conv_prev, dn_s0, dn_conv_prev, pool_prev, k_prev, v_prev, k_all, v_all, p, layer):
    y_conv, y_pool, dn_in, sb_in, (conv_new, dn_conv_new, pool_new), k_all, v_all = _front(
        x, mod, pos0, conv_prev, dn_conv_prev, pool_prev, k_all, v_all, p, layer)
    y_dn, s_new = _dn_mixer(*dn_in, dn_s0, p["dn_norm_g"])
    y_sb = _sb_mixer(*sb_in, k_prev, v_prev)
    x = _out_ffn(x, y_conv, y_dn, y_pool, y_sb, mod, p["norm_ffn"], p["w_out"], p["ffn_w_gate"],
                 p["ffn_w_up"], p["ffn_w_down"], layer)
    return x, conv_new, s_new, dn_conv_new, pool_new, k_all, v_all


def kernel(x_prompt, x_sample, c_prompt, c_sample, cache_conv, state_dn, cache_dn_conv, cache_pool, cache_sb_k, cache_sb_v, w_ada, b_ada, norm_mix, norm_ffn, w_in, w_out, conv_dw_w, conv_dw_b, conv_ln_g, conv_ln_b, dn_conv_w, dn_a_log, dn_dt_bias, dn_norm_g, pool_w, pool_scale, sb_q_norm, sb_k_norm, ffn_w_gate, ffn_w_up, ffn_w_down):
    depth = w_ada.shape[0]
    bp = x_prompt.shape[0]
    bs = x_sample.shape[0]
    past = cache_sb_k.shape[3]
    dt = x_prompt.dtype

    mod_all = _ada_modulation(jnp.concatenate([c_prompt, c_sample], axis=0), w_ada, b_ada)
    mod_all = mod_all.reshape(depth, bp + bs, N_MOD, D_MODEL)
    w_in_p = _pack_w_in(w_in)
    w_out_b, w_gate_b, w_up_b, w_down_b = (w.astype(BF16) for w in (w_out, ffn_w_gate, ffn_w_up, ffn_w_down))

    zeros_p = dict(
        conv=jnp.zeros((bp, CONV_W - 1, GROUP_W), dt),
        s0=jnp.zeros((bp, DN_HEADS, DN_HEAD_DIM, DN_HEAD_DIM), F32),
        dn_conv=jnp.zeros((bp, DN_CONV - 1, 3 * GROUP_W), dt),
        pool=jnp.zeros((bp, POOL_PREFIX, GROUP_W), dt),
        kv=jnp.zeros((bp, SB_HEADS, 0, SB_HEAD_DIM), dt))

    xp, xs = x_prompt, x_sample
    new_p = [[] for _ in range(4)]
    new_s = [[] for _ in range(4)]
    cache_shape = lambda x: (depth, x.shape[0], SB_HEADS, x.shape[1], SB_HEAD_DIM)
    sbk_p, sbv_p = jnp.zeros(cache_shape(xp), F32), jnp.zeros(cache_shape(xp), F32)
    sbk_s, sbv_s = jnp.zeros(cache_shape(xs), F32), jnp.zeros(cache_shape(xs), F32)
    for l in range(depth):
        row = lambda a: a[l].reshape(1, -1)
        p = dict(norm_mix=row(norm_mix), norm_ffn=row(norm_ffn), w_in=w_in_p, w_out=w_out_b,
                 conv_dw_w=conv_dw_w[l], conv_dw_b=row(conv_dw_b), conv_ln_g=row(conv_ln_g), conv_ln_b=row(conv_ln_b),
                 dn_conv_w=dn_conv_w[l], dn_a_log=dn_a_log[l], dn_dt_bias=dn_dt_bias[l], dn_norm_g=dn_norm_g[l],
                 pool_w=_pool_block_diag(pool_w[l]), pool_scale=row(pool_scale),
                 sb_q_norm=row(sb_q_norm), sb_k_norm=row(sb_k_norm),
                 ffn_w_gate=w_gate_b, ffn_w_up=w_up_b, ffn_w_down=w_down_b)
        xp, *sp, sbk_p, sbv_p = _trunk_layer(xp, mod_all[l, :bp], 0, zeros_p["conv"], zeros_p["s0"], zeros_p["dn_conv"],
                                             zeros_p["pool"], zeros_p["kv"], zeros_p["kv"], sbk_p, sbv_p, p, l)
        xs, *ss, sbk_s, sbv_s = _trunk_layer(xs, mod_all[l, bp:], past, cache_conv[l], state_dn[l], cache_dn_conv[l],
                                             cache_pool[l], cache_sb_k[l], cache_sb_v[l], sbk_s, sbv_s, p, l)
        for i in range(4):
            new_p[i].append(sp[i])
            new_s[i].append(ss[i])
    conv_p, dn_p, dnconv_p, pool_p = [jnp.stack(a) for a in new_p]
    conv_s, dn_s, dnconv_s, pool_s = [jnp.stack(a) for a in new_s]
    return (xp, xs, conv_p, conv_s, dn_p, dn_s, dnconv_p, dnconv_s, pool_p, pool_s, sbk_p, sbk_s, sbv_p, sbv_s)
```

```python
import functools

import jax
import jax.numpy as jnp
from jax import lax
from jax.experimental import pallas as pl
from jax.experimental.pallas import tpu as pltpu

D_MODEL = 1024
GROUP_W = 256
CONV_W = 31
DN_HEADS = 4
DN_HEAD_DIM = 64
DN_CONV = 4
CHUNK = 64
POOL_PREFIX = 15
SB_HEADS = 4
SB_HEAD_DIM = 64
SB_NEG = -1e30
D_FF = 2816
N_MOD = 6
EPS = 1e-6

V7X_LANES = 128
V7X_SUBLANES = 8
V7X_MXU_DIM = 256
V7X_VMEM_BYTES = 64 * 1024 * 1024
VMEM_LIMIT = V7X_VMEM_BYTES * 7 // 8

ROW_TILE = 512
FF_SPLIT = 2
CONV_PAD = 32
DN_PAD = 8
POOL_PAD = 16
AB_W = V7X_LANES
DN_UNROLL = 4
SB_BLOCK = V7X_MXU_DIM
SB_UNROLL = 2
LOG2_E = 1.4426950408889634
SQ_PARTS = 1
IN_WIDTHS = (2 * GROUP_W, 3 * GROUP_W, GROUP_W, AB_W, GROUP_W, 3 * GROUP_W)

F32 = jnp.float32
BF16 = jnp.bfloat16


def _cparams(*sem):
    return pltpu.CompilerParams(dimension_semantics=sem, vmem_limit_bytes=VMEM_LIMIT)


def _dot(a, b):
    return jnp.dot(a, b, preferred_element_type=F32)


def _dot_nt(a, b):
    return lax.dot_general(a, b, (((1,), (1,)), ((), ())), preferred_element_type=F32)


def _dot_tn(a, b):
    return lax.dot_general(a, b, (((0,), (0,)), ((), ())), preferred_element_type=F32)


def _split(x, parts):
    out = []
    for _ in range(parts):
        p = x.astype(BF16)
        out.append(p)
        x = x - p.astype(F32)
    return out


def _dot_sel_rhs(a, sel, parts):
    pieces = _split(a, parts)
    out = _dot(pieces[0], sel)
    for p in pieces[1:]:
        out = out + _dot(p, sel)
    return out


def _dot_sel_lhs(sel, b, parts):
    pieces = _split(b, parts)
    out = _dot(sel, pieces[0])
    for p in pieces[1:]:
        out = out + _dot(sel, p)
    return out


def _sigmoid(x):
    return 1.0 / (1.0 + jnp.exp(-x))


def _softplus(x):
    return jnp.maximum(x, 0.0) + jnp.log1p(jnp.exp(-jnp.abs(x)))


def _row_tiles(batch, length):
    if length >= ROW_TILE:
        assert length % ROW_TILE == 0
        return 1, ROW_TILE
    tb = max(1, min(batch, ROW_TILE // length))
    while batch % tb:
        tb -= 1
    return tb, length


def _const_spec(shape, layer=None):
    nd = len(shape)
    if layer is None:
        return pl.BlockSpec(shape, lambda *_: (0,) * nd, pipeline_mode=pl.Buffered(1))
    return pl.BlockSpec((None,) + shape, lambda *_: (layer,) + (0,) * nd, pipeline_mode=pl.Buffered(1))


def _unwritten(shape, dtype, name):
    return pl.pallas_call(
        lambda o_ref: None,
        out_specs=pl.BlockSpec(memory_space=pl.ANY),
        out_shape=jax.ShapeDtypeStruct(shape, dtype),
        name="unwritten_" + name,
    )()


def _head_ones(n, dtype):
    r = lax.broadcasted_iota(jnp.int32, (n, n), 0) // DN_HEAD_DIM
    c = lax.broadcasted_iota(jnp.int32, (n, n), 1) // DN_HEAD_DIM
    return (r == c).astype(dtype)


def _ada_kernel(c_ref, w_ref, b_ref, o_ref):
    c = c_ref[...]
    a = (c * _sigmoid(c)).astype(BF16)
    o_ref[0] = _dot(a, w_ref[0].astype(BF16)) + b_ref[0]


def _ada_modulation(c_all, w_ada, b_ada):
    depth, d, n = w_ada.shape
    nb = c_all.shape[0]
    tn = 1536
    assert n % tn == 0
    return pl.pallas_call(
        _ada_kernel,
        grid=(depth, n // tn),
        in_specs=[pl.BlockSpec((nb, d), lambda l, j: (0, 0)),
                  pl.BlockSpec((1, d, tn), lambda l, j: (l, 0, j)),
                  pl.BlockSpec((1, 1, tn), lambda l, j: (l, 0, j))],
        out_specs=pl.BlockSpec((1, nb, tn), lambda l, j: (l, 0, j)),
        out_shape=jax.ShapeDtypeStruct((depth, nb, n), F32),
        compiler_params=_cparams("arbitrary", "arbitrary"),
        name="ada_modulation",
    )(c_all, w_ada, b_ada.reshape(depth, 1, n))


def _conv_front(u, w_ref, b_ref, g_ref, beta_ref, y_ref, tail_ref, full_scr, shift_scr, bi):
    tl = u.shape[0]
    full_scr[bi, CONV_PAD:CONV_PAD + tl, :] = u[:, :GROUP_W] * _sigmoid(u[:, GROUP_W:])
    sub = min(tl, 64)
    first = CONV_PAD - (CONV_W - 1)
    span = shift_scr.shape[1]
    for phase in range(1, V7X_SUBLANES):
        shift_scr[phase - 1] = full_scr[bi, phase:phase + span, :]
    for r in range(0, tl, sub):
        acc = jnp.zeros((sub, GROUP_W), F32)
        for k in range(CONV_W):
            phase, lo = (first + k) % V7X_SUBLANES, (first + k) // V7X_SUBLANES * V7X_SUBLANES + r
            rows = full_scr[bi, lo:lo + sub, :] if phase == 0 else shift_scr[phase - 1, lo:lo + sub, :]
            acc = acc + w_ref[k:k + 1, :] * rows
        y = acc + b_ref[...]
        mu = jnp.mean(y, axis=-1, keepdims=True)
        yc = y - mu
        var = jnp.mean(yc * yc, axis=-1, keepdims=True)
        yn = yc * lax.rsqrt(var + EPS) * g_ref[...] + beta_ref[...]
        y_ref[bi, r:r + sub, :] = (yn * _sigmoid(yn)).astype(y_ref.dtype)
    tail = full_scr[bi, tl:tl + CONV_PAD, :]
    tail_ref[bi] = tail
    full_scr[bi, 0:CONV_PAD, :] = tail


def _pool_front(cur, w_ref, scale_ref, y_ref, tail_ref, full_scr, bi, pos):
    tl = cur.shape[0]
    full_scr[bi, POOL_PAD:POOL_PAD + tl, :] = cur
    lane = lax.broadcasted_iota(jnp.int32, (tl, GROUP_W), 1)
    row = lax.broadcasted_iota(jnp.int32, (tl, GROUP_W), 0)
    group = lane // (GROUP_W // 4)
    acc = cur
    win = None
    for i in range(1, 16):
        acc = acc + full_scr[bi, POOL_PAD - i:POOL_PAD - i + tl, :]
        if i in (1, 3, 7, 15):
            gi = (1, 3, 7, 15).index(i)
            win = acc if win is None else jnp.where(group >= gi, acc, win)
    width = jnp.left_shift(2, group)
    cnt = jnp.minimum(pos + row + 1, width).astype(F32)
    pooled = win / cnt - cur
    y = _dot(pooled.astype(BF16), w_ref[...]) * scale_ref[...]
    y_ref[bi] = y.astype(y_ref.dtype)
    tail = full_scr[bi, tl:tl + POOL_PAD, :]
    tail_ref[bi] = tail
    full_scr[bi, 0:POOL_PAD, :] = tail


def _dn_front(u, cw_ref, tail_ref, full_scr, bi):
    tl = u.shape[0]
    full_scr[bi, DN_PAD:DN_PAD + tl, :] = u
    first = DN_PAD - (DN_CONV - 1)
    acc = cw_ref[0:1, :] * full_scr[bi, first:first + tl, :]
    for k in range(1, DN_CONV):
        acc = acc + cw_ref[k:k + 1, :] * full_scr[bi, first + k:first + k + tl, :]
    tail = full_scr[bi, tl:tl + DN_PAD, :]
    tail_ref[bi] = tail
    full_scr[bi, 0:DN_PAD, :] = tail
    return acc * _sigmoid(acc)


def _front_kernel(x_ref, mod_ref, ng_ref, w_ref,
                  cprev_ref, cw_ref, cb_ref, cg_ref, cbeta_ref,
                  pprev_ref, pw_ref, pscale_ref,
                  dprev_ref, dcw_ref, alog_ref, dtb_ref,
                  qg_ref, kg_ref, k_all_ref, v_all_ref,
                  yc_ref, ctail_ref, yp_ref, ptail_ref,
                  dq_ref, dk_ref, dv_ref, dg_ref, db_ref, gate_ref, dtail_ref,
                  sq_ref, skt_ref, svb_ref, sk_ref, sv_ref,
                  cfull, cshift, pfull, dfull, *, pos0):
    del k_all_ref, v_all_ref
    tb, tl, d = x_ref.shape
    m = tb * tl
    gw = GROUP_W
    t = pl.program_id(1)

    @pl.when(t == 0)
    def _():
        cfull[:, 0:CONV_PAD, :] = cprev_ref[...]
        pfull[:, 0:POOL_PAD, :] = pprev_ref[...]
        dfull[:, 0:DN_PAD, :] = dprev_ref[...]

    x = x_ref[...]
    y = x * lax.rsqrt(jnp.mean(x * x, axis=-1, keepdims=True) + EPS) * ng_ref[...]
    hb = (y * (1.0 + mod_ref[:, 1:2, :]) + mod_ref[:, 0:1, :]).reshape(m, d).astype(BF16)
    starts = [sum(IN_WIDTHS[:i]) for i in range(len(IN_WIDTHS))]
    proj = lambda i: _dot(hb, w_ref[:, starts[i]:starts[i] + IN_WIDTHS[i]])
    seqs = [slice(bi * tl, (bi + 1) * tl) for bi in range(tb)]

    u_conv = proj(0)
    for bi, rows in enumerate(seqs):
        _conv_front(u_conv[rows], cw_ref, cb_ref, cg_ref, cbeta_ref, yc_ref, ctail_ref, cfull, cshift, bi)

    u_qkv = proj(1)
    qkv = jnp.concatenate([_dn_front(u_qkv[rows], dcw_ref, dtail_ref, dfull, bi)
                           for bi, rows in enumerate(seqs)], axis=0)
    ones_bd = _head_ones(gw, BF16)
    q = qkv[:, 0:gw]
    k = qkv[:, gw:2 * gw]
    dq_ref[...] = (q * lax.rsqrt(_dot_sel_rhs(q * q, ones_bd, SQ_PARTS) + EPS) * (DN_HEAD_DIM ** -0.5)).reshape(tb, tl, gw)
    dk_ref[...] = (k * lax.rsqrt(_dot_sel_rhs(k * k, ones_bd, SQ_PARTS) + EPS)).reshape(tb, tl, gw)
    dv_ref[...] = qkv[:, 2 * gw:3 * gw].reshape(tb, tl, gw)

    gate_ref[...] = proj(2).reshape(tb, tl, gw)

    ab = proj(3)
    g_all = -jnp.exp(alog_ref[...]) * _softplus(ab + dtb_ref[...])
    lane = lax.broadcasted_iota(jnp.int32, ab.shape, 1)
    gb = jnp.where(lane < DN_HEADS, g_all, _sigmoid(ab))
    er = lax.broadcasted_iota(jnp.int32, (AB_W, 2 * gw), 0)
    ec = lax.broadcasted_iota(jnp.int32, (AB_W, 2 * gw), 1)
    expand = (er == jnp.where(ec < gw, ec // DN_HEAD_DIM, DN_HEADS + (ec - gw) // DN_HEAD_DIM)).astype(BF16)
    gbx = _dot_sel_rhs(gb, expand, 3)
    dg_ref[...] = gbx[:, 0:gw].reshape(tb, tl, gw)
    db_ref[...] = gbx[:, gw:2 * gw].reshape(tb, tl, gw)

    u_pool = proj(4)
    for bi, rows in enumerate(seqs):
        _pool_front(u_pool[rows], pw_ref, pscale_ref, yp_ref, ptail_ref, pfull, bi, pos0 + t * tl)

    u_sb = proj(5)
    sq = u_sb[:, 0:gw]
    sk = u_sb[:, gw:2 * gw]
    sv = u_sb[:, 2 * gw:3 * gw]
    inv_d = 1.0 / SB_HEAD_DIM
    qn = sq * lax.rsqrt(_dot_sel_rhs(sq * sq, ones_bd, SQ_PARTS) * inv_d + EPS) * qg_ref[...]
    kn = sk * lax.rsqrt(_dot_sel_rhs(sk * sk, ones_bd, SQ_PARTS) * inv_d + EPS) * kg_ref[...]
    sq_ref[...] = (qn * (SB_HEAD_DIM ** -0.5 * LOG2_E)).astype(BF16).reshape(tb, tl, gw)
    svb_ref[...] = sv.astype(BF16).reshape(tb, tl, gw)
    for bi, rows in enumerate(seqs):
        skt_ref[bi] = kn[rows].T.astype(BF16)
        for h in range(SB_HEADS):
            hs = slice(h * SB_HEAD_DIM, (h + 1) * SB_HEAD_DIM)
            sk_ref[bi, h] = kn[rows, hs]
            sv_ref[bi, h] = sv[rows, hs]


def _front(x, mod, pos0, conv_prev, dn_conv_prev, pool_prev, k_all, v_all, p, layer):
    b, l, d = x.shape
    tb, tl = _row_tiles(b, l)
    gw, gw3 = GROUP_W, 3 * GROUP_W
    cprev = jnp.pad(conv_prev, ((0, 0), (CONV_PAD - (CONV_W - 1), 0), (0, 0)))
    pprev = jnp.pad(pool_prev, ((0, 0), (POOL_PAD - POOL_PREFIX, 0), (0, 0)))
    dprev = jnp.pad(dn_conv_prev, ((0, 0), (DN_PAD - (DN_CONV - 1), 0), (0, 0)))
    alog = jnp.pad(p["dn_a_log"], (0, AB_W - DN_HEADS)).reshape(1, AB_W)
    dtb = jnp.pad(p["dn_dt_bias"], (0, AB_W - DN_HEADS)).reshape(1, AB_W)
    qg = jnp.tile(p["sb_q_norm"], (1, SB_HEADS))
    kg = jnp.tile(p["sb_k_norm"], (1, SB_HEADS))
    assert k_all.shape[1:] == (b, SB_HEADS, l, SB_HEAD_DIM) and v_all.shape == k_all.shape
    rows = lambda n: pl.BlockSpec((tb, tl, n), lambda i, t: (i, t, 0))
    per_seq = lambda r, n: pl.BlockSpec((tb, r, n), lambda i, t: (i, 0, 0))
    cache = pl.BlockSpec((None, tb, SB_HEADS, tl, SB_HEAD_DIM), lambda i, t: (layer, i, 0, t, 0))
    f32_rows = jax.ShapeDtypeStruct((b, l, gw), F32)
    bf_rows = jax.ShapeDtypeStruct((b, l, gw), BF16)
    out = pl.pallas_call(
        functools.partial(_front_kernel, pos0=pos0),
        grid=(b // tb, l // tl),
        in_specs=[rows(d), per_seq(N_MOD, d), _const_spec((1, d)), _const_spec((d, sum(IN_WIDTHS)), layer),
                  per_seq(CONV_PAD, gw), _const_spec((CONV_W, gw)), _const_spec((1, gw)), _const_spec((1, gw)),
                  _const_spec((1, gw)),
                  per_seq(POOL_PAD, gw), _const_spec((gw, gw)), _const_spec((1, gw)),
                  per_seq(DN_PAD, gw3), _const_spec((DN_CONV, gw3)), _const_spec((1, AB_W)), _const_spec((1, AB_W)),
                  _const_spec((1, gw)), _const_spec((1, gw)),
                  pl.BlockSpec(memory_space=pl.ANY), pl.BlockSpec(memory_space=pl.ANY)],
        out_specs=[rows(gw), per_seq(CONV_PAD, gw), rows(gw), per_seq(POOL_PAD, gw),
                   rows(gw), rows(gw), rows(gw), rows(gw), rows(gw), rows(gw), per_seq(DN_PAD, gw3),
                   rows(gw), pl.BlockSpec((tb, gw, tl), lambda i, t: (i, 0, t)), rows(gw), cache, cache],
        out_shape=[bf_rows, jax.ShapeDtypeStruct((b, CONV_PAD, gw), F32),
                   bf_rows, jax.ShapeDtypeStruct((b, POOL_PAD, gw), F32),
                   f32_rows, f32_rows, f32_rows, f32_rows, f32_rows, f32_rows,
                   jax.ShapeDtypeStruct((b, DN_PAD, gw3), F32),
                   bf_rows, jax.ShapeDtypeStruct((b, gw, l), BF16), bf_rows,
                   jax.ShapeDtypeStruct(k_all.shape, F32), jax.ShapeDtypeStruct(v_all.shape, F32)],
        input_output_aliases={18: 14, 19: 15},
        scratch_shapes=[pltpu.VMEM((tb, CONV_PAD + tl, gw), F32),
                        pltpu.VMEM((V7X_SUBLANES - 1, CONV_PAD - V7X_SUBLANES + tl, gw), F32),
                        pltpu.VMEM((tb, POOL_PAD + tl, gw), F32),
                        pltpu.VMEM((tb, DN_PAD + tl, gw3), F32)],
        compiler_params=_cparams("arbitrary", "arbitrary"),
        name="front",
    )(x, mod, p["norm_mix"], p["w_in"],
      cprev, p["conv_dw_w"], p["conv_dw_b"], p["conv_ln_g"], p["conv_ln_b"],
      pprev, p["pool_w"], p["pool_scale"],
      dprev, p["dn_conv_w"], alog, dtb, qg, kg, k_all, v_all)
    (y_conv, ctail, y_pool, ptail, dq, dk, dv, dg, db, gate, dtail, sq, skt, svb, k_all, v_all) = out
    tails = (ctail[:, CONV_PAD - (CONV_W - 1):], dtail[:, DN_PAD - (DN_CONV - 1):], ptail[:, POOL_PAD - POOL_PREFIX:])
    return y_conv, y_pool, (dq, dk, dv, dg, db, gate), (sq, skt, svb), tails, k_all, v_all


def _unit_lower_inverses(lmats):
    c = lmats[0].shape[0]
    eye = (lax.broadcasted_iota(jnp.int32, (c, c), 0) == lax.broadcasted_iota(jnp.int32, (c, c), 1)).astype(F32)
    xs = [eye - m for m in lmats]
    ps = []
    for m in lmats:
        mh, ml = _split(m, 2)
        r1 = _dot(jnp.concatenate([mh, ml], axis=0), mh)
        ps.append(r1[0:c] + r1[c:2 * c] + _dot(mh, ml))
    span = 2
    while True:
        span *= 2
        last = span >= c
        nxt_x, nxt_p = [], []
        for x, p in zip(xs, ps):
            xh, xl = _split(x, 2)
            ph, pl_ = _split(p, 2)
            if last:
                r1 = _dot(jnp.concatenate([xh, xl], axis=0), ph)
                nxt_x.append(x + (r1[0:c] + r1[c:2 * c] + _dot(xh, pl_)))
            else:
                r1 = _dot(jnp.concatenate([xh, xl, ph, pl_], axis=0), ph)
                r2 = _dot(jnp.concatenate([xh, ph], axis=0), pl_)
                nxt_x.append(x + (r1[0:c] + r1[c:2 * c] + r2[0:c]))
                nxt_p.append(r1[2 * c:3 * c] + r1[3 * c:4 * c] + r2[c:2 * c])
        xs, ps = nxt_x, nxt_p
        if last:
            return xs


def _dn_kernel(q_ref, k_ref, v_ref, g_ref, b_ref, gate_ref, s0_ref, ng_ref, y_ref, s_ref, o_scr):
    tl = q_ref.shape[1]
    gw = GROUP_W
    hd = DN_HEAD_DIM

    @pl.when(pl.program_id(1) == 0)
    def _():
        s_ref[0] = s0_ref[0]

    ones_bd = _head_ones(gw, BF16)
    ci = lax.broadcasted_iota(jnp.int32, (CHUNK, CHUNK), 0)
    cj = lax.broadcasted_iota(jnp.int32, (CHUNK, CHUNK), 1)
    causal = ci >= cj
    strict = ci > cj
    tri = causal.astype(BF16)
    n_sub = min(DN_UNROLL, tl // CHUNK)
    heads = [slice(h * hd, (h + 1) * hd) for h in range(DN_HEADS)]

    def chunk_group(c, carry):
        pre = []
        lmats, attns = [], []
        for s in range(n_sub):
            rows = pl.ds(pl.multiple_of((c * n_sub + s) * CHUNK, CHUNK), CHUNK)
            qc, kc, vc, bc = q_ref[0, rows, :], k_ref[0, rows, :], v_ref[0, rows, :], b_ref[0, rows, :]
            gcb = _dot_sel_lhs(tri, g_ref[0, rows, :], 3)
            eg = jnp.exp(gcb)
            glast = gcb[CHUNK - 1:CHUNK, :]
            kb = kc * bc
            pre.append(dict(rows=rows, qc=qc, kc=kc, kb=kb, qd=qc * eg, ku=kc * jnp.exp(glast - gcb),
                            wr=kb * eg, vb=vc * bc, eglast=jnp.exp(glast)))
            for hs in heads:
                gh = gcb[:, hs]
                diff = gh - gh.T
                decay = jnp.where(causal, jnp.exp(jnp.where(causal, diff, 0.0)), 0.0)
                both = _dot_nt(jnp.concatenate([kb[:, hs], qc[:, hs]], axis=0).astype(BF16), kc[:, hs].astype(BF16))
                lmats.append(jnp.where(strict, both[0:CHUNK] * decay, 0.0))
                attns.append((both[CHUNK:2 * CHUNK] * decay).astype(BF16))
        tinvs = _unit_lower_inverses(lmats)
        sols = []
        for s in range(n_sub):
            p = pre[s]
            for h, hs in enumerate(heads):
                th, tlo = _split(tinvs[s * DN_HEADS + h], 2)
                rh, rlo = _split(jnp.concatenate([p["vb"][:, hs], p["wr"][:, hs]], axis=1), 2)
                r1 = _dot(jnp.concatenate([th, tlo], axis=0), rh)
                sols.append(r1[0:CHUNK] + r1[CHUNK:2 * CHUNK] + _dot(th, rlo))
        states = [s_ref[0, h] for h in range(DN_HEADS)]
        for s in range(n_sub):
            p = pre[s]
            outs = []
            for h, hs in enumerate(heads):
                i = s * DN_HEADS + h
                ws = _dot(jnp.concatenate([sols[i][:, hd:], p["qd"][:, hs]], axis=0).astype(BF16), states[h].astype(BF16))
                v_new = (sols[i][:, :hd] - ws[0:CHUNK]).astype(BF16)
                outs.append(ws[CHUNK:2 * CHUNK] + _dot(attns[i], v_new))
                states[h] = states[h] * p["eglast"][:, hs] + _dot_tn(p["ku"][:, hs].astype(BF16), v_new)
            o_scr[p["rows"], :] = jnp.concatenate(outs, axis=1)
        for h in range(DN_HEADS):
            s_ref[0, h] = states[h]
        return carry

    lax.fori_loop(0, tl // (CHUNK * n_sub), chunk_group, 0)

    o = o_scr[...]
    ms = _dot_sel_rhs(o * o, ones_bd, SQ_PARTS) * (1.0 / hd)
    gate = gate_ref[0]
    y_ref[0] = (o * lax.rsqrt(ms + EPS) * ng_ref[...] * (gate * _sigmoid(gate))).astype(y_ref.dtype)


def _dn_mixer(q, k, v, g, beta, gate, s0, norm_g):
    b, l, _ = q.shape
    assert l % CHUNK == 0
    tl = min(l, ROW_TILE)
    assert (tl // CHUNK) % min(DN_UNROLL, tl // CHUNK) == 0
    ng_p = jnp.tile(norm_g, DN_HEADS).reshape(1, GROUP_W)
    rows = pl.BlockSpec((1, tl, GROUP_W), lambda i, t: (i, t, 0))
    state_spec = pl.BlockSpec((1, DN_HEADS, DN_HEAD_DIM, DN_HEAD_DIM), lambda i, t: (i, 0, 0, 0))
    return pl.pallas_call(
        _dn_kernel,
        grid=(b, l // tl),
        in_specs=[rows] * 6 + [state_spec, _const_spec((1, GROUP_W))],
        out_specs=[rows, state_spec],
        out_shape=[jax.ShapeDtypeStruct((b, l, GROUP_W), BF16),
                   jax.ShapeDtypeStruct((b, DN_HEADS, DN_HEAD_DIM, DN_HEAD_DIM), F32)],
        scratch_shapes=[pltpu.VMEM((tl, GROUP_W), F32)],
        compiler_params=_cparams("arbitrary", "arbitrary"),
        name="deltanet_mixer",
    )(q, k, v, g, beta, gate, s0, ng_p)


def _sb_kernel(q_ref, kt_ref, v_ref, o_ref, later_scr, acc_scr, *, q_off, qb, kb):
    gw = GROUP_W
    q0 = q_off + pl.program_id(1) * qb
    nk = (q0 + qb - 1) // kb + 1
    head_of_lane = lax.broadcasted_iota(jnp.int32, (qb, gw), 1) // SB_HEAD_DIM
    q = q_ref[0].astype(F32)
    qh = [jnp.where(head_of_lane == h, q, 0.0).astype(BF16) for h in range(SB_HEADS)]
    head_of_vlane = lax.broadcasted_iota(jnp.int32, (kb, gw), 1) // SB_HEAD_DIM
    vmask = [(head_of_vlane == h).astype(BF16) for h in range(SB_HEADS)]
    r = lax.broadcasted_iota(jnp.int32, (kb, kb), 0)
    c = lax.broadcasted_iota(jnp.int32, (kb, kb), 1)
    suffix = (r >= c).astype(BF16)
    qpos = q0 + lax.broadcasted_iota(jnp.int32, (qb, kb), 0)
    kcol = lax.broadcasted_iota(jnp.int32, (qb, kb), 1)
    later_scr[...] = jnp.zeros_like(later_scr)
    acc_scr[...] = jnp.zeros_like(acc_scr)
    sign = jnp.uint32(0x80000000)

    def block(j):
        return pl.ds(pl.multiple_of(j * kb, kb), kb)

    def scores(js, masked=False):
        zs = [_dot(qh[h], kt_ref[0, :, block(j)]) for j in js for h in range(SB_HEADS)]
        if masked:
            zs = [jnp.where(js[i // SB_HEADS] * kb + kcol < qpos, z, SB_NEG) for i, z in enumerate(zs)]
        return zs

    def weights(zs):
        css = []
        for z in zs:
            neg_abs = lax.bitcast_convert_type(lax.bitcast_convert_type(z, jnp.uint32) | sign, F32)
            sp = jnp.maximum(z, 0.0) + jnp.log(1.0 + jnp.exp2(neg_abs)) * LOG2_E
            css.append(_dot(sp.astype(BF16), suffix))
        laters = [later_scr[h] for h in range(SB_HEADS)]
        parts = []
        for i, (z, cs) in enumerate(zip(zs, css)):
            h = i % SB_HEADS
            wide = jnp.concatenate([laters[h]] * (kb // V7X_LANES), axis=1)
            parts.append(jnp.exp2(z - cs - wide).astype(BF16))
            laters[h] = laters[h] + jnp.broadcast_to(cs[:, 0:1], laters[h].shape)
        for h in range(SB_HEADS):
            later_scr[h] = laters[h]
        return jnp.concatenate(parts, axis=1)

    def values(js):
        return jnp.concatenate([v_ref[0, block(j), :] * vmask[h] for j in js for h in range(SB_HEADS)], axis=0)

    def direct(js, masked):
        acc_scr[...] += _dot(weights(scores(js, masked)), values(js))

    direct([nk - 1], True)
    rest = nk - 1
    odd = rest % SB_UNROLL

    @pl.when(odd == 1)
    def _():
        direct([nk - 2], False)

    j0 = nk - 2 - odd

    def body(i, carry):
        direct([j0 - SB_UNROLL * i - s for s in range(SB_UNROLL)], False)
        return carry

    lax.fori_loop(0, rest // SB_UNROLL, body, 0)
    o_ref[0] = acc_scr[...].astype(o_ref.dtype)


def _sb_attention(q, kt_all, v_all, q_off):
    b, lq, gw = q.shape
    lk = v_all.shape[1]
    qb = min(SB_BLOCK, lq)
    kb = SB_BLOCK
    assert lq % qb == 0 and lk % kb == 0 and q_off % kb == 0 and kb % qb == 0 and SB_UNROLL == 2
    assert lk >= q_off + lq
    return pl.pallas_call(
        functools.partial(_sb_kernel, q_off=q_off, qb=qb, kb=kb),
        grid=(b, lq // qb),
        in_specs=[pl.BlockSpec((1, qb, gw), lambda i, t: (i, t, 0)),
                  pl.BlockSpec((1, gw, lk), lambda i, t: (i, 0, 0)),
                  pl.BlockSpec((1, lk, gw), lambda i, t: (i, 0, 0))],
        out_specs=pl.BlockSpec((1, qb, gw), lambda i, t: (i, t, 0)),
        out_shape=jax.ShapeDtypeStruct((b, lq, gw), BF16),
        scratch_shapes=[pltpu.VMEM((SB_HEADS, qb, V7X_LANES), F32), pltpu.VMEM((qb, gw), F32)],
        compiler_params=_cparams("arbitrary", "arbitrary"),
        name="sb_attention",
    )(q, kt_all, v_all)


def _sb_mixer(q, kt, vb, k_prev, v_prev):
    b, heads, past, d = k_prev.shape
    lq = q.shape[1]
    pad = (-(past + lq)) % SB_BLOCK
    if past or pad:
        kt_prev = jnp.swapaxes(k_prev, 2, 3).reshape(b, heads * d, past).astype(BF16)
        v_prev_rows = jnp.swapaxes(v_prev, 1, 2).reshape(b, past, heads * d).astype(BF16)
        kt = jnp.pad(jnp.concatenate([kt_prev, kt], axis=2), ((0, 0), (0, 0), (0, pad)))
        vb = jnp.pad(jnp.concatenate([v_prev_rows, vb], axis=1), ((0, 0), (0, pad), (0, 0)))
    return _sb_attention(q, kt, vb, past)


def _out_ffn_kernel(x_ref, yc_ref, yd_ref, yp_ref, ys_ref, mod_ref, g_ref, wo_ref, wg_ref, wu_ref, wd_ref, o_ref):
    tb, tl, d = x_ref.shape
    m = tb * tl
    gw = GROUP_W
    ycat = jnp.concatenate([r[...].reshape(m, gw) for r in (yc_ref, yd_ref, yp_ref, ys_ref)], axis=1)
    mix = _dot(ycat, wo_ref[...])
    x1 = x_ref[...] + mod_ref[:, 2:3, :] * mix.reshape(tb, tl, d)
    y = x1 * lax.rsqrt(jnp.mean(x1 * x1, axis=-1, keepdims=True) + EPS) * g_ref[...]
    h2 = (y * (1.0 + mod_ref[:, 4:5, :]) + mod_ref[:, 3:4, :]).reshape(m, d).astype(BF16)
    ff = D_FF // FF_SPLIT
    acc = jnp.zeros((m, d), F32)
    for c in range(FF_SPLIT):
        gate = _dot(h2, wg_ref[:, c * ff:(c + 1) * ff])
        up = _dot(h2, wu_ref[:, c * ff:(c + 1) * ff])
        act = (gate * _sigmoid(gate) * up).astype(BF16)
        acc = acc + _dot(act, wd_ref[c * ff:(c + 1) * ff, :])
    o_ref[...] = x1 + mod_ref[:, 5:6, :] * acc.reshape(tb, tl, d)


def _out_ffn(x, y_conv, y_dn, y_pool, y_sb, mod, norm_g, w_out, w_gate, w_up, w_down, layer):
    b, l, d = x.shape
    tb, tl = _row_tiles(b, l)
    assert (D_FF // FF_SPLIT) % V7X_LANES == 0
    row_spec = lambda n: pl.BlockSpec((tb, tl, n), lambda i, t: (i, t, 0))
    return pl.pallas_call(
        _out_ffn_kernel,
        grid=(b // tb, l // tl),
        in_specs=[row_spec(d), row_spec(GROUP_W), row_spec(GROUP_W), row_spec(GROUP_W), row_spec(GROUP_W),
                  pl.BlockSpec((tb, N_MOD, d), lambda i, t: (i, 0, 0)),
                  _const_spec((1, d)),
                  _const_spec((d, d), layer), _const_spec((d, D_FF), layer),
                  _const_spec((d, D_FF), layer), _const_spec((D_FF, d), layer)],
        out_specs=row_spec(d),
        out_shape=jax.ShapeDtypeStruct((b, l, d), F32),
        compiler_params=_cparams("arbitrary", "arbitrary"),
        name="out_ffn",
    )(x, y_conv, y_dn, y_pool, y_sb, mod, norm_g, w_out, w_gate, w_up, w_down)


def _pack_w_in(w_in):
    c0 = 2 * GROUP_W + 3 * GROUP_W + GROUP_W
    ab = w_in[..., c0:c0 + 2 * DN_HEADS]
    ab = jnp.pad(ab, ((0, 0), (0, 0), (0, AB_W - 2 * DN_HEADS)))
    return jnp.concatenate([w_in[..., :c0], ab, w_in[..., c0 + 2 * DN_HEADS:]], axis=-1).astype(BF16)


def _pool_block_diag(pool_w):
    g, n, _ = pool_w.shape
    out = jnp.zeros((g * n, g * n), pool_w.dtype)
    for i in range(g):
        out = out.at[i * n:(i + 1) * n, i * n:(i + 1) * n].set(pool_w[i])
    return out.astype(BF16)


def _trunk_layer(x, mod, pos0, conv_prev, dn_s0, dn_conv_prev, pool_prev, k_prev, v_prev, k_all, v_all, p, layer):
    y_conv, y_pool, dn_in, sb_in, (conv_new, dn_conv_new, pool_new), k_all, v_all = _front(
        x, mod, pos0, conv_prev, dn_conv_prev, pool_prev, k_all, v_all, p, layer)
    y_dn, s_new = _dn_mixer(*dn_in, dn_s0, p["dn_norm_g"])
    y_sb = _sb_mixer(*sb_in, k_prev, v_prev)
    x = _out_ffn(x, y_conv, y_dn, y_pool, y_sb, mod, p["norm_ffn"], p["w_out"], p["ffn_w_gate"],
                 p["ffn_w_up"], p["ffn_w_down"], layer)
    return x, conv_new, s_new, dn_conv_new, pool_new, k_all, v_all


def kernel(x_prompt, x_sample, c_prompt, c_sample, cache_conv, state_dn, cache_dn_conv, cache_pool, cache_sb_k, cache_sb_v, w_ada, b_ada, norm_mix, norm_ffn, w_in, w_out, conv_dw_w, conv_dw_b, conv_ln_g, conv_ln_b, dn_conv_w, dn_a_log, dn_dt_bias, dn_norm_g, pool_w, pool_scale, sb_q_norm, sb_k_norm, ffn_w_gate, ffn_w_up, ffn_w_down):
    depth = w_ada.shape[0]
    bp = x_prompt.shape[0]
    bs = x_sample.shape[0]
    past = cache_sb_k.shape[3]
    dt = x_prompt.dtype

    mod_all = _ada_modulation(jnp.concatenate([c_prompt, c_sample], axis=0), w_ada, b_ada)
    mod_all = mod_all.reshape(depth, bp + bs, N_MOD, D_MODEL)
    w_in_p = _pack_w_in(w_in)
    w_out_b, w_gate_b, w_up_b, w_down_b = (w.astype(BF16) for w in (w_out, ffn_w_gate, ffn_w_up, ffn_w_down))

    zeros_p = dict(
        conv=jnp.zeros((bp, CONV_W - 1, GROUP_W), dt),
        s0=jnp.zeros((bp, DN_HEADS, DN_HEAD_DIM, DN_HEAD_DIM), F32),
        dn_conv=jnp.zeros((bp, DN_CONV - 1, 3 * GROUP_W), dt),
        pool=jnp.zeros((bp, POOL_PREFIX, GROUP_W), dt),
        kv=jnp.zeros((bp, SB_HEADS, 0, SB_HEAD_DIM), dt))

    xp, xs = x_prompt, x_sample
    new_p = [[] for _ in range(4)]
    new_s = [[] for _ in range(4)]
    cache_buf = lambda x, name: _unwritten((depth, x.shape[0], SB_HEADS, x.shape[1], SB_HEAD_DIM), F32, name)
    sbk_p, sbv_p = cache_buf(xp, "k_prompt"), cache_buf(xp, "v_prompt")
    sbk_s, sbv_s = cache_buf(xs, "k_sample"), cache_buf(xs, "v_sample")
    for l in range(depth):
        row = lambda a: a[l].reshape(1, -1)
        p = dict(norm_mix=row(norm_mix), norm_ffn=row(norm_ffn), w_in=w_in_p, w_out=w_out_b,
                 conv_dw_w=conv_dw_w[l], conv_dw_b=row(conv_dw_b), conv_ln_g=row(conv_ln_g), conv_ln_b=row(conv_ln_b),
                 dn_conv_w=dn_conv_w[l], dn_a_log=dn_a_log[l], dn_dt_bias=dn_dt_bias[l], dn_norm_g=dn_norm_g[l],
                 pool_w=_pool_block_diag(pool_w[l]), pool_scale=row(pool_scale),
                 sb_q_norm=row(sb_q_norm), sb_k_norm=row(sb_k_norm),
                 ffn_w_gate=w_gate_b, ffn_w_up=w_up_b, ffn_w_down=w_down_b)
        xp, *sp, sbk_p, sbv_p = _trunk_layer(xp, mod_all[l, :bp], 0, zeros_p["conv"], zeros_p["s0"], zeros_p["dn_conv"],
                                             zeros_p["pool"], zeros_p["kv"], zeros_p["kv"], sbk_p, sbv_p, p, l)
        xs, *ss, sbk_s, sbv_s = _trunk_layer(xs, mod_all[l, bp:], past, cache_conv[l], state_dn[l], cache_dn_conv[l],
                                             cache_pool[l], cache_sb_k[l], cache_sb_v[l], sbk_s, sbv_s, p, l)
        for i in range(4):
            new_p[i].append(sp[i])
            new_s[i].append(ss[i])
    conv_p, dn_p, dnconv_p, pool_p = [jnp.stack(a) for a in new_p]
    conv_s, dn_s, dnconv_s, pool_s = [jnp.stack(a) for a in new_s]
    return (xp, xs, conv_p, conv_s, dn_p, dn_s, dnconv_p, dnconv_s, pool_p, pool_s, sbk_p, sbk_s, sbv_p, sbv_s)
```

```python
import functools

import jax
import jax.numpy as jnp
from jax import lax
from jax.experimental import pallas as pl
from jax.experimental.pallas import tpu as pltpu

D_MODEL = 1024
GROUP_W = 256
CONV_W = 31
DN_HEADS = 4
DN_HEAD_DIM = 64
DN_CONV = 4
CHUNK = 64
POOL_PREFIX = 15
SB_HEADS = 4
SB_HEAD_DIM = 64
SB_NEG = -1e30
D_FF = 2816
N_MOD = 6
EPS = 1e-6

V7X_LANES = 128
V7X_SUBLANES = 8
V7X_MXU_DIM = 256
V7X_VMEM_BYTES = 64 * 1024 * 1024
VMEM_LIMIT = V7X_VMEM_BYTES * 7 // 8

ROW_TILE = 512
FF_SPLIT = 2
CONV_PAD = 32
DN_PAD = 8
POOL_PAD = 16
AB_W = V7X_LANES
DN_UNROLL = 4
SB_BLOCK = V7X_MXU_DIM
SB_UNROLL = 4
LOG2_E = 1.4426950408889634
SQ_PARTS = 1
IN_WIDTHS = (2 * GROUP_W, 3 * GROUP_W, GROUP_W, AB_W, GROUP_W, 3 * GROUP_W)

F32 = jnp.float32
BF16 = jnp.bfloat16


def _cparams(*sem):
    return pltpu.CompilerParams(dimension_semantics=sem, vmem_limit_bytes=VMEM_LIMIT)


def _dot(a, b):
    return jnp.dot(a, b, preferred_element_type=F32)


def _dot_nt(a, b):
    return lax.dot_general(a, b, (((1,), (1,)), ((), ())), preferred_element_type=F32)


def _dot_tn(a, b):
    return lax.dot_general(a, b, (((0,), (0,)), ((), ())), preferred_element_type=F32)


def _split(x, parts):
    out = []
    for _ in range(parts):
        p = x.astype(BF16)
        out.append(p)
        x = x - p.astype(F32)
    return out


def _dot_sel_rhs(a, sel, parts):
    pieces = _split(a, parts)
    out = _dot(pieces[0], sel)
    for p in pieces[1:]:
        out = out + _dot(p, sel)
    return out


def _dot_sel_lhs(sel, b, parts):
    pieces = _split(b, parts)
    out = _dot(sel, pieces[0])
    for p in pieces[1:]:
        out = out + _dot(sel, p)
    return out


def _sigmoid(x):
    return 1.0 / (1.0 + jnp.exp(-x))


def _softplus(x):
    return jnp.maximum(x, 0.0) + jnp.log1p(jnp.exp(-jnp.abs(x)))


def _row_tiles(batch, length):
    if length >= ROW_TILE:
        assert length % ROW_TILE == 0
        return 1, ROW_TILE
    tb = max(1, min(batch, ROW_TILE // length))
    while batch % tb:
        tb -= 1
    return tb, length


def _const_spec(shape, layer=None):
    nd = len(shape)
    if layer is None:
        return pl.BlockSpec(shape, lambda *_: (0,) * nd, pipeline_mode=pl.Buffered(1))
    return pl.BlockSpec((None,) + shape, lambda *_: (layer,) + (0,) * nd, pipeline_mode=pl.Buffered(1))


def _unwritten(shape, dtype, name):
    return pl.pallas_call(
        lambda o_ref: None,
        out_specs=pl.BlockSpec(memory_space=pl.ANY),
        out_shape=jax.ShapeDtypeStruct(shape, dtype),
        name="unwritten_" + name,
    )()


def _head_ones(n, dtype):
    r = lax.broadcasted_iota(jnp.int32, (n, n), 0) // DN_HEAD_DIM
    c = lax.broadcasted_iota(jnp.int32, (n, n), 1) // DN_HEAD_DIM
    return (r == c).astype(dtype)


def _ada_kernel(c_ref, w_ref, b_ref, o_ref):
    c = c_ref[...]
    a = (c * _sigmoid(c)).astype(BF16)
    o_ref[0] = _dot(a, w_ref[0].astype(BF16)) + b_ref[0]


def _ada_modulation(c_all, w_ada, b_ada):
    depth, d, n = w_ada.shape
    nb = c_all.shape[0]
    tn = 1536
    assert n % tn == 0
    return pl.pallas_call(
        _ada_kernel,
        grid=(depth, n // tn),
        in_specs=[pl.BlockSpec((nb, d), lambda l, j: (0, 0)),
                  pl.BlockSpec((1, d, tn), lambda l, j: (l, 0, j)),
                  pl.BlockSpec((1, 1, tn), lambda l, j: (l, 0, j))],
        out_specs=pl.BlockSpec((1, nb, tn), lambda l, j: (l, 0, j)),
        out_shape=jax.ShapeDtypeStruct((depth, nb, n), F32),
        compiler_params=_cparams("arbitrary", "arbitrary"),
        name="ada_modulation",
    )(c_all, w_ada, b_ada.reshape(depth, 1, n))


def _conv_front(u, w_ref, b_ref, g_ref, beta_ref, y_ref, tail_ref, full_scr, shift_scr, bi):
    tl = u.shape[0]
    full_scr[bi, CONV_PAD:CONV_PAD + tl, :] = u[:, :GROUP_W] * _sigmoid(u[:, GROUP_W:])
    sub = min(tl, 64)
    first = CONV_PAD - (CONV_W - 1)
    span = shift_scr.shape[1]
    for phase in range(1, V7X_SUBLANES):
        shift_scr[phase - 1] = full_scr[bi, phase:phase + span, :]
    for r in range(0, tl, sub):
        acc = jnp.zeros((sub, GROUP_W), F32)
        for k in range(CONV_W):
            phase, lo = (first + k) % V7X_SUBLANES, (first + k) // V7X_SUBLANES * V7X_SUBLANES + r
            rows = full_scr[bi, lo:lo + sub, :] if phase == 0 else shift_scr[phase - 1, lo:lo + sub, :]
            acc = acc + w_ref[k:k + 1, :] * rows
        y = acc + b_ref[...]
        mu = jnp.mean(y, axis=-1, keepdims=True)
        yc = y - mu
        var = jnp.mean(yc * yc, axis=-1, keepdims=True)
        yn = yc * lax.rsqrt(var + EPS) * g_ref[...] + beta_ref[...]
        y_ref[bi, r:r + sub, :] = (yn * _sigmoid(yn)).astype(y_ref.dtype)
    tail = full_scr[bi, tl:tl + CONV_PAD, :]
    tail_ref[bi] = tail
    full_scr[bi, 0:CONV_PAD, :] = tail


def _pool_front(cur, w_ref, scale_ref, y_ref, tail_ref, full_scr, bi, pos):
    tl = cur.shape[0]
    full_scr[bi, POOL_PAD:POOL_PAD + tl, :] = cur
    lane = lax.broadcasted_iota(jnp.int32, (tl, GROUP_W), 1)
    row = lax.broadcasted_iota(jnp.int32, (tl, GROUP_W), 0)
    group = lane // (GROUP_W // 4)
    acc = cur
    win = None
    for i in range(1, 16):
        acc = acc + full_scr[bi, POOL_PAD - i:POOL_PAD - i + tl, :]
        if i in (1, 3, 7, 15):
            gi = (1, 3, 7, 15).index(i)
            win = acc if win is None else jnp.where(group >= gi, acc, win)
    width = jnp.left_shift(2, group)
    cnt = jnp.minimum(pos + row + 1, width).astype(F32)
    pooled = win / cnt - cur
    y = _dot(pooled.astype(BF16), w_ref[...]) * scale_ref[...]
    y_ref[bi] = y.astype(y_ref.dtype)
    tail = full_scr[bi, tl:tl + POOL_PAD, :]
    tail_ref[bi] = tail
    full_scr[bi, 0:POOL_PAD, :] = tail


def _dn_front(u, cw_ref, tail_ref, full_scr, bi):
    tl = u.shape[0]
    full_scr[bi, DN_PAD:DN_PAD + tl, :] = u
    first = DN_PAD - (DN_CONV - 1)
    acc = cw_ref[0:1, :] * full_scr[bi, first:first + tl, :]
    for k in range(1, DN_CONV):
        acc = acc + cw_ref[k:k + 1, :] * full_scr[bi, first + k:first + k + tl, :]
    tail = full_scr[bi, tl:tl + DN_PAD, :]
    tail_ref[bi] = tail
    full_scr[bi, 0:DN_PAD, :] = tail
    return acc * _sigmoid(acc)


def _front_kernel(x_ref, mod_ref, ng_ref, w_ref,
                  cprev_ref, cw_ref, cb_ref, cg_ref, cbeta_ref,
                  pprev_ref, pw_ref, pscale_ref,
                  dprev_ref, dcw_ref, alog_ref, dtb_ref,
                  qg_ref, kg_ref, k_all_ref, v_all_ref,
                  yc_ref, ctail_ref, yp_ref, ptail_ref,
                  dq_ref, dk_ref, dv_ref, dg_ref, db_ref, gate_ref, dtail_ref,
                  sq_ref, skt_ref, svb_ref, sk_ref, sv_ref,
                  cfull, cshift, pfull, dfull, *, pos0):
    del k_all_ref, v_all_ref
    tb, tl, d = x_ref.shape
    m = tb * tl
    gw = GROUP_W
    t = pl.program_id(1)

    @pl.when(t == 0)
    def _():
        cfull[:, 0:CONV_PAD, :] = cprev_ref[...]
        pfull[:, 0:POOL_PAD, :] = pprev_ref[...]
        dfull[:, 0:DN_PAD, :] = dprev_ref[...]

    x = x_ref[...]
    y = x * lax.rsqrt(jnp.mean(x * x, axis=-1, keepdims=True) + EPS) * ng_ref[...]
    hb = (y * (1.0 + mod_ref[:, 1:2, :]) + mod_ref[:, 0:1, :]).reshape(m, d).astype(BF16)
    starts = [sum(IN_WIDTHS[:i]) for i in range(len(IN_WIDTHS))]
    proj = lambda i: _dot(hb, w_ref[:, starts[i]:starts[i] + IN_WIDTHS[i]])
    seqs = [slice(bi * tl, (bi + 1) * tl) for bi in range(tb)]

    u_conv = proj(0)
    for bi, rows in enumerate(seqs):
        _conv_front(u_conv[rows], cw_ref, cb_ref, cg_ref, cbeta_ref, yc_ref, ctail_ref, cfull, cshift, bi)

    u_qkv = proj(1)
    qkv = jnp.concatenate([_dn_front(u_qkv[rows], dcw_ref, dtail_ref, dfull, bi)
                           for bi, rows in enumerate(seqs)], axis=0)
    ones_bd = _head_ones(gw, BF16)
    q = qkv[:, 0:gw]
    k = qkv[:, gw:2 * gw]
    dq_ref[...] = (q * lax.rsqrt(_dot_sel_rhs(q * q, ones_bd, SQ_PARTS) + EPS) * (DN_HEAD_DIM ** -0.5)).reshape(tb, tl, gw)
    dk_ref[...] = (k * lax.rsqrt(_dot_sel_rhs(k * k, ones_bd, SQ_PARTS) + EPS)).reshape(tb, tl, gw)
    dv_ref[...] = qkv[:, 2 * gw:3 * gw].reshape(tb, tl, gw)

    gate_ref[...] = proj(2).reshape(tb, tl, gw)

    ab = proj(3)
    g_all = -jnp.exp(alog_ref[...]) * _softplus(ab + dtb_ref[...])
    lane = lax.broadcasted_iota(jnp.int32, ab.shape, 1)
    gb = jnp.where(lane < DN_HEADS, g_all, _sigmoid(ab))
    er = lax.broadcasted_iota(jnp.int32, (AB_W, 2 * gw), 0)
    ec = lax.broadcasted_iota(jnp.int32, (AB_W, 2 * gw), 1)
    expand = (er == jnp.where(ec < gw, ec // DN_HEAD_DIM, DN_HEADS + (ec - gw) // DN_HEAD_DIM)).astype(BF16)
    gbx = _dot_sel_rhs(gb, expand, 3)
    dg_ref[...] = gbx[:, 0:gw].reshape(tb, tl, gw)
    db_ref[...] = gbx[:, gw:2 * gw].reshape(tb, tl, gw)

    u_pool = proj(4)
    for bi, rows in enumerate(seqs):
        _pool_front(u_pool[rows], pw_ref, pscale_ref, yp_ref, ptail_ref, pfull, bi, pos0 + t * tl)

    u_sb = proj(5)
    sq = u_sb[:, 0:gw]
    sk = u_sb[:, gw:2 * gw]
    sv = u_sb[:, 2 * gw:3 * gw]
    inv_d = 1.0 / SB_HEAD_DIM
    qn = sq * lax.rsqrt(_dot_sel_rhs(sq * sq, ones_bd, SQ_PARTS) * inv_d + EPS) * qg_ref[...]
    kn = sk * lax.rsqrt(_dot_sel_rhs(sk * sk, ones_bd, SQ_PARTS) * inv_d + EPS) * kg_ref[...]
    sq_ref[...] = (qn * (SB_HEAD_DIM ** -0.5 * LOG2_E)).astype(BF16).reshape(tb, tl, gw)
    svb_ref[...] = sv.astype(BF16).reshape(tb, tl, gw)
    for bi, rows in enumerate(seqs):
        skt_ref[bi] = kn[rows].T.astype(BF16)
        for h in range(SB_HEADS):
            hs = slice(h * SB_HEAD_DIM, (h + 1) * SB_HEAD_DIM)
            sk_ref[bi, h] = kn[rows, hs]
            sv_ref[bi, h] = sv[rows, hs]


def _front(x, mod, pos0, conv_prev, dn_conv_prev, pool_prev, k_all, v_all, p, layer):
    b, l, d = x.shape
    tb, tl = _row_tiles(b, l)
    gw, gw3 = GROUP_W, 3 * GROUP_W
    cprev = jnp.pad(conv_prev, ((0, 0), (CONV_PAD - (CONV_W - 1), 0), (0, 0)))
    pprev = jnp.pad(pool_prev, ((0, 0), (POOL_PAD - POOL_PREFIX, 0), (0, 0)))
    dprev = jnp.pad(dn_conv_prev, ((0, 0), (DN_PAD - (DN_CONV - 1), 0), (0, 0)))
    alog = jnp.pad(p["dn_a_log"], (0, AB_W - DN_HEADS)).reshape(1, AB_W)
    dtb = jnp.pad(p["dn_dt_bias"], (0, AB_W - DN_HEADS)).reshape(1, AB_W)
    qg = jnp.tile(p["sb_q_norm"], (1, SB_HEADS))
    kg = jnp.tile(p["sb_k_norm"], (1, SB_HEADS))
    assert k_all.shape[1:] == (b, SB_HEADS, l, SB_HEAD_DIM) and v_all.shape == k_all.shape
    rows = lambda n: pl.BlockSpec((tb, tl, n), lambda i, t: (i, t, 0))
    per_seq = lambda r, n: pl.BlockSpec((tb, r, n), lambda i, t: (i, 0, 0))
    cache = pl.BlockSpec((None, tb, SB_HEADS, tl, SB_HEAD_DIM), lambda i, t: (layer, i, 0, t, 0))
    f32_rows = jax.ShapeDtypeStruct((b, l, gw), F32)
    bf_rows = jax.ShapeDtypeStruct((b, l, gw), BF16)
    out = pl.pallas_call(
        functools.partial(_front_kernel, pos0=pos0),
        grid=(b // tb, l // tl),
        in_specs=[rows(d), per_seq(N_MOD, d), _const_spec((1, d)), _const_spec((d, sum(IN_WIDTHS)), layer),
                  per_seq(CONV_PAD, gw), _const_spec((CONV_W, gw)), _const_spec((1, gw)), _const_spec((1, gw)),
                  _const_spec((1, gw)),
                  per_seq(POOL_PAD, gw), _const_spec((gw, gw)), _const_spec((1, gw)),
                  per_seq(DN_PAD, gw3), _const_spec((DN_CONV, gw3)), _const_spec((1, AB_W)), _const_spec((1, AB_W)),
                  _const_spec((1, gw)), _const_spec((1, gw)),
                  pl.BlockSpec(memory_space=pl.ANY), pl.BlockSpec(memory_space=pl.ANY)],
        out_specs=[rows(gw), per_seq(CONV_PAD, gw), rows(gw), per_seq(POOL_PAD, gw),
                   rows(gw), rows(gw), rows(gw), rows(gw), rows(gw), rows(gw), per_seq(DN_PAD, gw3),
                   rows(gw), pl.BlockSpec((tb, gw, tl), lambda i, t: (i, 0, t)), rows(gw), cache, cache],
        out_shape=[bf_rows, jax.ShapeDtypeStruct((b, CONV_PAD, gw), F32),
                   bf_rows, jax.ShapeDtypeStruct((b, POOL_PAD, gw), F32),
                   f32_rows, f32_rows, f32_rows, f32_rows, f32_rows, f32_rows,
                   jax.ShapeDtypeStruct((b, DN_PAD, gw3), F32),
                   bf_rows, jax.ShapeDtypeStruct((b, gw, l), BF16), bf_rows,
                   jax.ShapeDtypeStruct(k_all.shape, F32), jax.ShapeDtypeStruct(v_all.shape, F32)],
        input_output_aliases={18: 14, 19: 15},
        scratch_shapes=[pltpu.VMEM((tb, CONV_PAD + tl, gw), F32),
                        pltpu.VMEM((V7X_SUBLANES - 1, CONV_PAD - V7X_SUBLANES + tl, gw), F32),
                        pltpu.VMEM((tb, POOL_PAD + tl, gw), F32),
                        pltpu.VMEM((tb, DN_PAD + tl, gw3), F32)],
        compiler_params=_cparams("arbitrary", "arbitrary"),
        name="front",
    )(x, mod, p["norm_mix"], p["w_in"],
      cprev, p["conv_dw_w"], p["conv_dw_b"], p["conv_ln_g"], p["conv_ln_b"],
      pprev, p["pool_w"], p["pool_scale"],
      dprev, p["dn_conv_w"], alog, dtb, qg, kg, k_all, v_all)
    (y_conv, ctail, y_pool, ptail, dq, dk, dv, dg, db, gate, dtail, sq, skt, svb, k_all, v_all) = out
    tails = (ctail[:, CONV_PAD - (CONV_W - 1):], dtail[:, DN_PAD - (DN_CONV - 1):], ptail[:, POOL_PAD - POOL_PREFIX:])
    return y_conv, y_pool, (dq, dk, dv, dg, db, gate), (sq, skt, svb), tails, k_all, v_all


def _unit_lower_inverses(lmats):
    c = lmats[0].shape[0]
    eye = (lax.broadcasted_iota(jnp.int32, (c, c), 0) == lax.broadcasted_iota(jnp.int32, (c, c), 1)).astype(F32)
    xs = [eye - m for m in lmats]
    ps = []
    for m in lmats:
        mh, ml = _split(m, 2)
        r1 = _dot(jnp.concatenate([mh, ml], axis=0), mh)
        ps.append(r1[0:c] + r1[c:2 * c] + _dot(mh, ml))
    span = 2
    while True:
        span *= 2
        last = span >= c
        nxt_x, nxt_p = [], []
        for x, p in zip(xs, ps):
            xh, xl = _split(x, 2)
            ph, pl_ = _split(p, 2)
            if last:
                r1 = _dot(jnp.concatenate([xh, xl], axis=0), ph)
                nxt_x.append(x + (r1[0:c] + r1[c:2 * c] + _dot(xh, pl_)))
            else:
                r1 = _dot(jnp.concatenate([xh, xl, ph, pl_], axis=0), ph)
                r2 = _dot(jnp.concatenate([xh, ph], axis=0), pl_)
                nxt_x.append(x + (r1[0:c] + r1[c:2 * c] + r2[0:c]))
                nxt_p.append(r1[2 * c:3 * c] + r1[3 * c:4 * c] + r2[c:2 * c])
        xs, ps = nxt_x, nxt_p
        if last:
            return xs


def _dn_kernel(q_ref, k_ref, v_ref, g_ref, b_ref, gate_ref, s0_ref, ng_ref, y_ref, s_ref, o_scr):
    tl = q_ref.shape[1]
    gw = GROUP_W
    hd = DN_HEAD_DIM

    @pl.when(pl.program_id(1) == 0)
    def _():
        s_ref[0] = s0_ref[0]

    ones_bd = _head_ones(gw, BF16)
    ci = lax.broadcasted_iota(jnp.int32, (CHUNK, CHUNK), 0)
    cj = lax.broadcasted_iota(jnp.int32, (CHUNK, CHUNK), 1)
    causal = ci >= cj
    strict = ci > cj
    tri = causal.astype(BF16)
    n_sub = min(DN_UNROLL, tl // CHUNK)
    heads = [slice(h * hd, (h + 1) * hd) for h in range(DN_HEADS)]

    def chunk_group(c, carry):
        pre = []
        lmats, attns = [], []
        for s in range(n_sub):
            rows = pl.ds(pl.multiple_of((c * n_sub + s) * CHUNK, CHUNK), CHUNK)
            qc, kc, vc, bc = q_ref[0, rows, :], k_ref[0, rows, :], v_ref[0, rows, :], b_ref[0, rows, :]
            gcb = _dot_sel_lhs(tri, g_ref[0, rows, :], 3)
            eg = jnp.exp(gcb)
            glast = gcb[CHUNK - 1:CHUNK, :]
            kb = kc * bc
            pre.append(dict(rows=rows, qc=qc, kc=kc, kb=kb, qd=qc * eg, ku=kc * jnp.exp(glast - gcb),
                            wr=kb * eg, vb=vc * bc, eglast=jnp.exp(glast)))
            for hs in heads:
                gh = gcb[:, hs]
                diff = gh - gh.T
                decay = jnp.where(causal, jnp.exp(jnp.where(causal, diff, 0.0)), 0.0)
                both = _dot_nt(jnp.concatenate([kb[:, hs], qc[:, hs]], axis=0).astype(BF16), kc[:, hs].astype(BF16))
                lmats.append(jnp.where(strict, both[0:CHUNK] * decay, 0.0))
                attns.append((both[CHUNK:2 * CHUNK] * decay).astype(BF16))
        tinvs = _unit_lower_inverses(lmats)
        sols = []
        for s in range(n_sub):
            p = pre[s]
            for h, hs in enumerate(heads):
                th, tlo = _split(tinvs[s * DN_HEADS + h], 2)
                rh, rlo = _split(jnp.concatenate([p["vb"][:, hs], p["wr"][:, hs]], axis=1), 2)
                r1 = _dot(jnp.concatenate([th, tlo], axis=0), rh)
                sols.append(r1[0:CHUNK] + r1[CHUNK:2 * CHUNK] + _dot(th, rlo))
        states = [s_ref[0, h] for h in range(DN_HEADS)]
        for s in range(n_sub):
            p = pre[s]
            outs = []
            for h, hs in enumerate(heads):
                i = s * DN_HEADS + h
                ws = _dot(jnp.concatenate([sols[i][:, hd:], p["qd"][:, hs]], axis=0).astype(BF16), states[h].astype(BF16))
                v_new = (sols[i][:, :hd] - ws[0:CHUNK]).astype(BF16)
                outs.append(ws[CHUNK:2 * CHUNK] + _dot(attns[i], v_new))
                states[h] = states[h] * p["eglast"][:, hs] + _dot_tn(p["ku"][:, hs].astype(BF16), v_new)
            o_scr[p["rows"], :] = jnp.concatenate(outs, axis=1)
        for h in range(DN_HEADS):
            s_ref[0, h] = states[h]
        return carry

    lax.fori_loop(0, tl // (CHUNK * n_sub), chunk_group, 0)

    o = o_scr[...]
    ms = _dot_sel_rhs(o * o, ones_bd, SQ_PARTS) * (1.0 / hd)
    gate = gate_ref[0]
    y_ref[0] = (o * lax.rsqrt(ms + EPS) * ng_ref[...] * (gate * _sigmoid(gate))).astype(y_ref.dtype)


def _dn_mixer(q, k, v, g, beta, gate, s0, norm_g):
    b, l, _ = q.shape
    assert l % CHUNK == 0
    tl = min(l, ROW_TILE)
    assert (tl // CHUNK) % min(DN_UNROLL, tl // CHUNK) == 0
    ng_p = jnp.tile(norm_g, DN_HEADS).reshape(1, GROUP_W)
    rows = pl.BlockSpec((1, tl, GROUP_W), lambda i, t: (i, t, 0))
    state_spec = pl.BlockSpec((1, DN_HEADS, DN_HEAD_DIM, DN_HEAD_DIM), lambda i, t: (i, 0, 0, 0))
    return pl.pallas_call(
        _dn_kernel,
        grid=(b, l // tl),
        in_specs=[rows] * 6 + [state_spec, _const_spec((1, GROUP_W))],
        out_specs=[rows, state_spec],
        out_shape=[jax.ShapeDtypeStruct((b, l, GROUP_W), BF16),
                   jax.ShapeDtypeStruct((b, DN_HEADS, DN_HEAD_DIM, DN_HEAD_DIM), F32)],
        scratch_shapes=[pltpu.VMEM((tl, GROUP_W), F32)],
        compiler_params=_cparams("arbitrary", "arbitrary"),
        name="deltanet_mixer",
    )(q, k, v, g, beta, gate, s0, ng_p)


def _sb_kernel(q_ref, kt_ref, v_ref, o_ref, later_scr, acc_scr, *, q_off, qb, kb):
    gw = GROUP_W
    q0 = q_off + pl.program_id(1) * qb
    nk = (q0 + qb - 1) // kb + 1
    head_of_lane = lax.broadcasted_iota(jnp.int32, (qb, gw), 1) // SB_HEAD_DIM
    q = q_ref[0].astype(F32)
    qh = [jnp.where(head_of_lane == h, q, 0.0).astype(BF16) for h in range(SB_HEADS)]
    head_of_vlane = lax.broadcasted_iota(jnp.int32, (kb, gw), 1) // SB_HEAD_DIM
    vmask = [(head_of_vlane == h).astype(BF16) for h in range(SB_HEADS)]
    r = lax.broadcasted_iota(jnp.int32, (kb, kb), 0)
    c = lax.broadcasted_iota(jnp.int32, (kb, kb), 1)
    suffix = (r >= c).astype(BF16)
    qpos = q0 + lax.broadcasted_iota(jnp.int32, (qb, kb), 0)
    kcol = lax.broadcasted_iota(jnp.int32, (qb, kb), 1)
    later_scr[...] = jnp.zeros_like(later_scr)
    acc_scr[...] = jnp.zeros_like(acc_scr)
    sign = jnp.uint32(0x80000000)

    def block(j):
        return pl.ds(pl.multiple_of(j * kb, kb), kb)

    def scores(js, masked=False):
        zs = [_dot(qh[h], kt_ref[0, :, block(j)]) for j in js for h in range(SB_HEADS)]
        if masked:
            zs = [jnp.where(js[i // SB_HEADS] * kb + kcol < qpos, z, SB_NEG) for i, z in enumerate(zs)]
        return zs

    def weights(zs):
        css = []
        for z in zs:
            neg_abs = lax.bitcast_convert_type(lax.bitcast_convert_type(z, jnp.uint32) | sign, F32)
            sp = jnp.maximum(z, 0.0) + jnp.log(1.0 + jnp.exp2(neg_abs)) * LOG2_E
            css.append(_dot(sp.astype(BF16), suffix))
        laters = [later_scr[h] for h in range(SB_HEADS)]
        parts = []
        for i, (z, cs) in enumerate(zip(zs, css)):
            h = i % SB_HEADS
            wide = jnp.concatenate([laters[h]] * (kb // V7X_LANES), axis=1)
            parts.append(jnp.exp2(z - cs - wide).astype(BF16))
            laters[h] = laters[h] + jnp.broadcast_to(cs[:, 0:1], laters[h].shape)
        for h in range(SB_HEADS):
            later_scr[h] = laters[h]
        return jnp.concatenate(parts, axis=1)

    def values(js):
        return jnp.concatenate([v_ref[0, block(j), :] * vmask[h] for j in js for h in range(SB_HEADS)], axis=0)

    def direct(js, masked):
        w = weights(scores(js, masked))
        halves = [js] if len(js) == 1 else [js[:len(js) // 2], js[len(js) // 2:]]
        lo = 0
        for part, sub in enumerate(halves):
            hi = lo + len(sub) * SB_HEADS * kb
            acc_scr[part] += _dot(w[:, lo:hi], values(sub))
            lo = hi

    direct([nk - 1], True)
    rest = nk - 1
    j = nk - 2
    width = 1
    while width < SB_UNROLL:
        take = (rest % SB_UNROLL) & width

        @pl.when(take != 0)
        def _(j=j, width=width):
            direct([j - s for s in range(width)], False)

        j = j - take
        width *= 2

    def body(i, carry):
        direct([j - SB_UNROLL * i - s for s in range(SB_UNROLL)], False)
        return carry

    lax.fori_loop(0, rest // SB_UNROLL, body, 0)
    o_ref[0] = (acc_scr[0] + acc_scr[1]).astype(o_ref.dtype)


def _sb_attention(q, kt_all, v_all, q_off):
    b, lq, gw = q.shape
    lk = v_all.shape[1]
    qb = min(SB_BLOCK, lq)
    kb = SB_BLOCK
    assert lq % qb == 0 and lk % kb == 0 and q_off % kb == 0 and kb % qb == 0 and SB_UNROLL & (SB_UNROLL - 1) == 0
    assert lk >= q_off + lq
    return pl.pallas_call(
        functools.partial(_sb_kernel, q_off=q_off, qb=qb, kb=kb),
        grid=(b, lq // qb),
        in_specs=[pl.BlockSpec((1, qb, gw), lambda i, t: (i, t, 0)),
                  pl.BlockSpec((1, gw, lk), lambda i, t: (i, 0, 0)),
                  pl.BlockSpec((1, lk, gw), lambda i, t: (i, 0, 0))],
        out_specs=pl.BlockSpec((1, qb, gw), lambda i, t: (i, t, 0)),
        out_shape=jax.ShapeDtypeStruct((b, lq, gw), BF16),
        scratch_shapes=[pltpu.VMEM((SB_HEADS, qb, V7X_LANES), F32), pltpu.VMEM((2, qb, gw), F32)],
        compiler_params=_cparams("arbitrary", "arbitrary"),
        name="sb_attention",
    )(q, kt_all, v_all)


def _sb_mixer(q, kt, vb, k_prev, v_prev):
    b, heads, past, d = k_prev.shape
    lq = q.shape[1]
    pad = (-(past + lq)) % SB_BLOCK
    if past or pad:
        kt_prev = jnp.swapaxes(k_prev, 2, 3).reshape(b, heads * d, past).astype(BF16)
        v_prev_rows = jnp.swapaxes(v_prev, 1, 2).reshape(b, past, heads * d).astype(BF16)
        kt = jnp.pad(jnp.concatenate([kt_prev, kt], axis=2), ((0, 0), (0, 0), (0, pad)))
        vb = jnp.pad(jnp.concatenate([v_prev_rows, vb], axis=1), ((0, 0), (0, pad), (0, 0)))
    return _sb_attention(q, kt, vb, past)


def _out_ffn_kernel(x_ref, yc_ref, yd_ref, yp_ref, ys_ref, mod_ref, g_ref, wo_ref, wg_ref, wu_ref, wd_ref, o_ref):
    tb, tl, d = x_ref.shape
    m = tb * tl
    gw = GROUP_W
    ycat = jnp.concatenate([r[...].reshape(m, gw) for r in (yc_ref, yd_ref, yp_ref, ys_ref)], axis=1)
    mix = _dot(ycat, wo_ref[...])
    x1 = x_ref[...] + mod_ref[:, 2:3, :] * mix.reshape(tb, tl, d)
    y = x1 * lax.rsqrt(jnp.mean(x1 * x1, axis=-1, keepdims=True) + EPS) * g_ref[...]
    h2 = (y * (1.0 + mod_ref[:, 4:5, :]) + mod_ref[:, 3:4, :]).reshape(m, d).astype(BF16)
    ff = D_FF // FF_SPLIT
    acc = jnp.zeros((m, d), F32)
    for c in range(FF_SPLIT):
        gate = _dot(h2, wg_ref[:, c * ff:(c + 1) * ff])
        up = _dot(h2, wu_ref[:, c * ff:(c + 1) * ff])
        act = (gate * _sigmoid(gate) * up).astype(BF16)
        acc = acc + _dot(act, wd_ref[c * ff:(c + 1) * ff, :])
    o_ref[...] = x1 + mod_ref[:, 5:6, :] * acc.reshape(tb, tl, d)


def _out_ffn(x, y_conv, y_dn, y_pool, y_sb, mod, norm_g, w_out, w_gate, w_up, w_down, layer):
    b, l, d = x.shape
    tb, tl = _row_tiles(b, l)
    assert (D_FF // FF_SPLIT) % V7X_LANES == 0
    row_spec = lambda n: pl.BlockSpec((tb, tl, n), lambda i, t: (i, t, 0))
    return pl.pallas_call(
        _out_ffn_kernel,
        grid=(b // tb, l // tl),
        in_specs=[row_spec(d), row_spec(GROUP_W), row_spec(GROUP_W), row_spec(GROUP_W), row_spec(GROUP_W),
                  pl.BlockSpec((tb, N_MOD, d), lambda i, t: (i, 0, 0)),
                  _const_spec((1, d)),
                  _const_spec((d, d), layer), _const_spec((d, D_FF), layer),
                  _const_spec((d, D_FF), layer), _const_spec((D_FF, d), layer)],
        out_specs=row_spec(d),
        out_shape=jax.ShapeDtypeStruct((b, l, d), F32),
        compiler_params=_cparams("arbitrary", "arbitrary"),
        name="out_ffn",
    )(x, y_conv, y_dn, y_pool, y_sb, mod, norm_g, w_out, w_gate, w_up, w_down)


def _pack_w_in(w_in):
    c0 = 2 * GROUP_W + 3 * GROUP_W + GROUP_W
    ab = w_in[..., c0:c0 + 2 * DN_HEADS]
    ab = jnp.pad(ab, ((0, 0), (0, 0), (0, AB_W - 2 * DN_HEADS)))
    return jnp.concatenate([w_in[..., :c0], ab, w_in[..., c0 + 2 * DN_HEADS:]], axis=-1).astype(BF16)


def _pool_block_diag(pool_w):
    g, n, _ = pool_w.shape
    out = jnp.zeros((g * n, g * n), pool_w.dtype)
    for i in range(g):
        out = out.at[i * n:(i + 1) * n, i * n:(i + 1) * n].set(pool_w[i])
    return out.astype(BF16)


def _trunk_layer(x, mod, pos0, conv_prev, dn_s0, dn_conv_prev, pool_prev, k_prev, v_prev, k_all, v_all, p, layer):
    y_conv, y_pool, dn_in, sb_in, (conv_new, dn_conv_new, pool_new), k_all, v_all = _front(
        x, mod, pos0, conv_prev, dn_conv_prev, pool_prev, k_all, v_all, p, layer)
    y_dn, s_new = _dn_mixer(*dn_in, dn_s0, p["dn_norm_g"])
    y_sb = _sb_mixer(*sb_in, k_prev, v_prev)
    x = _out_ffn(x, y_conv, y_dn, y_pool, y_sb, mod, p["norm_ffn"], p["w_out"], p["ffn_w_gate"],
                 p["ffn_w_up"], p["ffn_w_down"], layer)
    return x, conv_new, s_new, dn_conv_new, pool_new, k_all, v_all


def kernel(x_prompt, x_sample, c_prompt, c_sample, cache_conv, state_dn, cache_dn_conv, cache_pool, cache_sb_k, cache_sb_v, w_ada, b_ada, norm_mix, norm_ffn, w_in, w_out, conv_dw_w, conv_dw_b, conv_ln_g, conv_ln_b, dn_conv_w, dn_a_log, dn_dt_bias, dn_norm_g, pool_w, pool_scale, sb_q_norm, sb_k_norm, ffn_w_gate, ffn_w_up, ffn_w_down):
    depth = w_ada.shape[0]
    bp = x_prompt.shape[0]
    bs = x_sample.shape[0]
    past = cache_sb_k.shape[3]
    dt = x_prompt.dtype

    mod_all = _ada_modulation(jnp.concatenate([c_prompt, c_sample], axis=0), w_ada, b_ada)
    mod_all = mod_all.reshape(depth, bp + bs, N_MOD, D_MODEL)
    w_in_p = _pack_w_in(w_in)
    w_out_b, w_gate_b, w_up_b, w_down_b = (w.astype(BF16) for w in (w_out, ffn_w_gate, ffn_w_up, ffn_w_down))

    zeros_p = dict(
        conv=jnp.zeros((bp, CONV_W - 1, GROUP_W), dt),
        s0=jnp.zeros((bp, DN_HEADS, DN_HEAD_DIM, DN_HEAD_DIM), F32),
        dn_conv=jnp.zeros((bp, DN_CONV - 1, 3 * GROUP_W), dt),
        pool=jnp.zeros((bp, POOL_PREFIX, GROUP_W), dt),
        kv=jnp.zeros((bp, SB_HEADS, 0, SB_HEAD_DIM), dt))

    xp, xs = x_prompt, x_sample
    new_p = [[] for _ in range(4)]
    new_s = [[] for _ in range(4)]
    cache_buf = lambda x, name: _unwritten((depth, x.shape[0], SB_HEADS, x.shape[1], SB_HEAD_DIM), F32, name)
    sbk_p, sbv_p = cache_buf(xp, "k_prompt"), cache_buf(xp, "v_prompt")
    sbk_s, sbv_s = cache_buf(xs, "k_sample"), cache_buf(xs, "v_sample")
    for l in range(depth):
        row = lambda a: a[l].reshape(1, -1)
        p = dict(norm_mix=row(norm_mix), norm_ffn=row(norm_ffn), w_in=w_in_p, w_out=w_out_b,
                 conv_dw_w=conv_dw_w[l], conv_dw_b=row(conv_dw_b), conv_ln_g=row(conv_ln_g), conv_ln_b=row(conv_ln_b),
                 dn_conv_w=dn_conv_w[l], dn_a_log=dn_a_log[l], dn_dt_bias=dn_dt_bias[l], dn_norm_g=dn_norm_g[l],
                 pool_w=_pool_block_diag(pool_w[l]), pool_scale=row(pool_scale),
                 sb_q_norm=row(sb_q_norm), sb_k_norm=row(sb_k_norm),
                 ffn_w_gate=w_gate_b, ffn_w_up=w_up_b, ffn_w_down=w_down_b)
        xp, *sp, sbk_p, sbv_p = _trunk_layer(xp, mod_all[l, :bp], 0, zeros_p["conv"], zeros_p["s0"], zeros_p["dn_conv"],
                                             zeros_p["pool"], zeros_p["kv"], zeros_p["kv"], sbk_p, sbv_p, p, l)
        xs, *ss, sbk_s, sbv_s = _trunk_layer(xs, mod_all[l, bp:], past, cache_conv[l], state_dn[l], cache_dn_conv[l],
                                             cache_pool[l], cache_sb_k[l], cache_sb_v[l], sbk_s, sbv_s, p, l)
        for i in range(4):
            new_p[i].append(sp[i])
            new_s[i].append(ss[i])
    conv_p, dn_p, dnconv_p, pool_p = [jnp.stack(a) for a in new_p]
    conv_s, dn_s, dnconv_s, pool_s = [jnp.stack(a) for a in new_s]
    return (xp, xs, conv_p, conv_s, dn_p, dn_s, dnconv_p, dnconv_s, pool_p, pool_s, sbk_p, sbk_s, sbv_p, sbv_s)
```

```python
import functools

import jax
import jax.numpy as jnp
from jax import lax
from jax.experimental import pallas as pl
from jax.experimental.pallas import tpu as pltpu

D_MODEL = 1024
GROUP_W = 256
CONV_W = 31
DN_HEADS = 4
DN_HEAD_DIM = 64
DN_CONV = 4
CHUNK = 64
POOL_WINDOWS = (2, 4, 8, 16)
POOL_PREFIX = POOL_WINDOWS[-1] - 1
SB_HEADS = 4
SB_HEAD_DIM = 64
SB_NEG = -1e30
D_FF = 2816
N_MOD = 6
EPS = 1e-6

V7X_LANES = 128
V7X_SUBLANES = 8
V7X_MXU_DIM = 256
V7X_VMEM_BYTES = 64 * 1024 * 1024
VMEM_LIMIT = V7X_VMEM_BYTES * 7 // 8

ROW_TILE = 512
FF_SPLIT = 2
CONV_PAD = 32
DN_PAD = 8
POOL_PAD = 16
AB_W = V7X_LANES
DN_UNROLL = 8
ADA_COL_TILE = 1536
CONV_SUB_ROWS = 64
SB_BLOCK = V7X_MXU_DIM
SB_UNROLL = 4
LOG2_E = 1.4426950408889634
SQ_PARTS = 1
IN_WIDTHS = (2 * GROUP_W, 3 * GROUP_W, GROUP_W, AB_W, GROUP_W, 3 * GROUP_W)

F32 = jnp.float32
BF16 = jnp.bfloat16


def _cparams(*sem):
    return pltpu.CompilerParams(dimension_semantics=sem, vmem_limit_bytes=VMEM_LIMIT)


def _dot(a, b):
    return jnp.dot(a, b, preferred_element_type=F32)


def _dot_nt(a, b):
    return lax.dot_general(a, b, (((1,), (1,)), ((), ())), preferred_element_type=F32)


def _dot_tn(a, b):
    return lax.dot_general(a, b, (((0,), (0,)), ((), ())), preferred_element_type=F32)


def _split(x, parts):
    out = []
    for _ in range(parts):
        p = x.astype(BF16)
        out.append(p)
        x = x - p.astype(F32)
    return out


def _dot_sel_rhs(a, sel, parts):
    pieces = _split(a, parts)
    out = _dot(pieces[0], sel)
    for p in pieces[1:]:
        out = out + _dot(p, sel)
    return out


def _dot_sel_lhs(sel, b, parts):
    pieces = _split(b, parts)
    out = _dot(sel, pieces[0])
    for p in pieces[1:]:
        out = out + _dot(sel, p)
    return out


def _sigmoid(x):
    return 1.0 / (1.0 + jnp.exp(-x))


def _softplus(x):
    return jnp.maximum(x, 0.0) + jnp.log1p(jnp.exp(-jnp.abs(x)))


def _row_tiles(batch, length):
    if length >= ROW_TILE:
        assert length % ROW_TILE == 0
        return 1, ROW_TILE
    tb = max(1, min(batch, ROW_TILE // length))
    while batch % tb:
        tb -= 1
    return tb, length


def _const_spec(shape, layer=None):
    nd = len(shape)
    if layer is None:
        return pl.BlockSpec(shape, lambda *_: (0,) * nd, pipeline_mode=pl.Buffered(1))
    return pl.BlockSpec((None,) + shape, lambda *_: (layer,) + (0,) * nd, pipeline_mode=pl.Buffered(1))


def _unwritten(shape, dtype, name):
    return pl.pallas_call(
        lambda o_ref: None,
        out_specs=pl.BlockSpec(memory_space=pl.ANY),
        out_shape=jax.ShapeDtypeStruct(shape, dtype),
        name="unwritten_" + name,
    )()


def _head_ones(n, dtype):
    r = lax.broadcasted_iota(jnp.int32, (n, n), 0) // DN_HEAD_DIM
    c = lax.broadcasted_iota(jnp.int32, (n, n), 1) // DN_HEAD_DIM
    return (r == c).astype(dtype)


def _ada_kernel(c_ref, w_ref, b_ref, o_ref):
    c = c_ref[...]
    a = (c * _sigmoid(c)).astype(BF16)
    o_ref[0] = _dot(a, w_ref[0].astype(BF16)) + b_ref[0]


def _ada_modulation(c_all, w_ada, b_ada):
    depth, d, n = w_ada.shape
    nb = c_all.shape[0]
    tn = ADA_COL_TILE
    assert n % tn == 0
    return pl.pallas_call(
        _ada_kernel,
        grid=(depth, n // tn),
        in_specs=[pl.BlockSpec((nb, d), lambda l, j: (0, 0)),
                  pl.BlockSpec((1, d, tn), lambda l, j: (l, 0, j)),
                  pl.BlockSpec((1, 1, tn), lambda l, j: (l, 0, j))],
        out_specs=pl.BlockSpec((1, nb, tn), lambda l, j: (l, 0, j)),
        out_shape=jax.ShapeDtypeStruct((depth, nb, n), F32),
        compiler_params=_cparams("arbitrary", "arbitrary"),
        name="ada_modulation",
    )(c_all, w_ada, b_ada.reshape(depth, 1, n))


def _conv_front(u, w_ref, b_ref, g_ref, beta_ref, y_ref, tail_ref, full_scr, shift_scr, bi):
    tl = u.shape[0]
    full_scr[bi, CONV_PAD:CONV_PAD + tl, :] = u[:, :GROUP_W] * _sigmoid(u[:, GROUP_W:])
    sub = min(tl, CONV_SUB_ROWS)
    first = CONV_PAD - (CONV_W - 1)
    span = shift_scr.shape[1]
    for phase in range(1, V7X_SUBLANES):
        shift_scr[phase - 1] = full_scr[bi, phase:phase + span, :]
    for r in range(0, tl, sub):
        acc = jnp.zeros((sub, GROUP_W), F32)
        for k in range(CONV_W):
            phase, lo = (first + k) % V7X_SUBLANES, (first + k) // V7X_SUBLANES * V7X_SUBLANES + r
            rows = full_scr[bi, lo:lo + sub, :] if phase == 0 else shift_scr[phase - 1, lo:lo + sub, :]
            acc = acc + w_ref[k:k + 1, :] * rows
        y = acc + b_ref[...]
        mu = jnp.mean(y, axis=-1, keepdims=True)
        yc = y - mu
        var = jnp.mean(yc * yc, axis=-1, keepdims=True)
        yn = yc * lax.rsqrt(var + EPS) * g_ref[...] + beta_ref[...]
        y_ref[bi, r:r + sub, :] = (yn * _sigmoid(yn)).astype(y_ref.dtype)
    tail = full_scr[bi, tl:tl + CONV_PAD, :]
    tail_ref[bi] = tail
    full_scr[bi, 0:CONV_PAD, :] = tail


def _pool_front(cur, w_ref, scale_ref, y_ref, tail_ref, full_scr, bi, pos):
    tl = cur.shape[0]
    full_scr[bi, POOL_PAD:POOL_PAD + tl, :] = cur
    lane = lax.broadcasted_iota(jnp.int32, (tl, GROUP_W), 1)
    row = lax.broadcasted_iota(jnp.int32, (tl, GROUP_W), 0)
    group = lane // (GROUP_W // len(POOL_WINDOWS))
    acc = cur
    win = None
    for i in range(1, POOL_WINDOWS[-1]):
        acc = acc + full_scr[bi, POOL_PAD - i:POOL_PAD - i + tl, :]
        if i + 1 in POOL_WINDOWS:
            gi = POOL_WINDOWS.index(i + 1)
            win = acc if win is None else jnp.where(group >= gi, acc, win)
    width = jnp.left_shift(POOL_WINDOWS[0], group)
    cnt = jnp.minimum(pos + row + 1, width).astype(F32)
    pooled = win / cnt - cur
    y = _dot(pooled.astype(BF16), w_ref[...]) * scale_ref[...]
    y_ref[bi] = y.astype(y_ref.dtype)
    tail = full_scr[bi, tl:tl + POOL_PAD, :]
    tail_ref[bi] = tail
    full_scr[bi, 0:POOL_PAD, :] = tail


def _dn_front(u, cw_ref, tail_ref, full_scr, bi):
    tl = u.shape[0]
    full_scr[bi, DN_PAD:DN_PAD + tl, :] = u
    first = DN_PAD - (DN_CONV - 1)
    acc = cw_ref[0:1, :] * full_scr[bi, first:first + tl, :]
    for k in range(1, DN_CONV):
        acc = acc + cw_ref[k:k + 1, :] * full_scr[bi, first + k:first + k + tl, :]
    tail = full_scr[bi, tl:tl + DN_PAD, :]
    tail_ref[bi] = tail
    full_scr[bi, 0:DN_PAD, :] = tail
    return acc * _sigmoid(acc)


def _front_kernel(x_ref, mod_ref, ng_ref, w_ref,
                  cprev_ref, cw_ref, cb_ref, cg_ref, cbeta_ref,
                  pprev_ref, pw_ref, pscale_ref,
                  dprev_ref, dcw_ref, alog_ref, dtb_ref,
                  qg_ref, kg_ref, k_all_ref, v_all_ref,
                  yc_ref, ctail_ref, yp_ref, ptail_ref,
                  dq_ref, dk_ref, dv_ref, dg_ref, db_ref, gate_ref, dtail_ref,
                  sq_ref, skt_ref, svb_ref, sk_ref, sv_ref,
                  cfull, cshift, pfull, dfull, *, pos0):
    del k_all_ref, v_all_ref
    tb, tl, d = x_ref.shape
    m = tb * tl
    gw = GROUP_W
    t = pl.program_id(1)

    @pl.when(t == 0)
    def _():
        cfull[:, 0:CONV_PAD, :] = cprev_ref[...]
        pfull[:, 0:POOL_PAD, :] = pprev_ref[...]
        dfull[:, 0:DN_PAD, :] = dprev_ref[...]

    x = x_ref[...]
    y = x * lax.rsqrt(jnp.mean(x * x, axis=-1, keepdims=True) + EPS) * ng_ref[...]
    hb = (y * (1.0 + mod_ref[:, 1:2, :]) + mod_ref[:, 0:1, :]).reshape(m, d).astype(BF16)
    starts = [sum(IN_WIDTHS[:i]) for i in range(len(IN_WIDTHS))]
    proj = lambda i: _dot(hb, w_ref[:, starts[i]:starts[i] + IN_WIDTHS[i]])
    seqs = [slice(bi * tl, (bi + 1) * tl) for bi in range(tb)]

    u_conv = proj(0)
    for bi, rows in enumerate(seqs):
        _conv_front(u_conv[rows], cw_ref, cb_ref, cg_ref, cbeta_ref, yc_ref, ctail_ref, cfull, cshift, bi)

    u_qkv = proj(1)
    qkv = jnp.concatenate([_dn_front(u_qkv[rows], dcw_ref, dtail_ref, dfull, bi)
                           for bi, rows in enumerate(seqs)], axis=0)
    ones_bd = _head_ones(gw, BF16)
    q = qkv[:, 0:gw]
    k = qkv[:, gw:2 * gw]
    dq_ref[...] = (q * lax.rsqrt(_dot_sel_rhs(q * q, ones_bd, SQ_PARTS) + EPS) * (DN_HEAD_DIM ** -0.5)).reshape(tb, tl, gw)
    dk_ref[...] = (k * lax.rsqrt(_dot_sel_rhs(k * k, ones_bd, SQ_PARTS) + EPS)).reshape(tb, tl, gw)
    dv_ref[...] = qkv[:, 2 * gw:3 * gw].reshape(tb, tl, gw)

    gate_ref[...] = proj(2).reshape(tb, tl, gw)

    ab = proj(3)
    g_all = -jnp.exp(alog_ref[...]) * _softplus(ab + dtb_ref[...])
    lane = lax.broadcasted_iota(jnp.int32, ab.shape, 1)
    gb = jnp.where(lane < DN_HEADS, g_all, _sigmoid(ab))
    er = lax.broadcasted_iota(jnp.int32, (AB_W, 2 * gw), 0)
    ec = lax.broadcasted_iota(jnp.int32, (AB_W, 2 * gw), 1)
    expand = (er == jnp.where(ec < gw, ec // DN_HEAD_DIM, DN_HEADS + (ec - gw) // DN_HEAD_DIM)).astype(BF16)
    gbx = _dot_sel_rhs(gb, expand, 3)
    dg_ref[...] = gbx[:, 0:gw].reshape(tb, tl, gw)
    db_ref[...] = gbx[:, gw:2 * gw].reshape(tb, tl, gw)

    u_pool = proj(4)
    for bi, rows in enumerate(seqs):
        _pool_front(u_pool[rows], pw_ref, pscale_ref, yp_ref, ptail_ref, pfull, bi, pos0 + t * tl)

    u_sb = proj(5)
    sq = u_sb[:, 0:gw]
    sk = u_sb[:, gw:2 * gw]
    sv = u_sb[:, 2 * gw:3 * gw]
    inv_d = 1.0 / SB_HEAD_DIM
    qn = sq * lax.rsqrt(_dot_sel_rhs(sq * sq, ones_bd, SQ_PARTS) * inv_d + EPS) * qg_ref[...]
    kn = sk * lax.rsqrt(_dot_sel_rhs(sk * sk, ones_bd, SQ_PARTS) * inv_d + EPS) * kg_ref[...]
    sq_ref[...] = (qn * (SB_HEAD_DIM ** -0.5 * LOG2_E)).astype(BF16).reshape(tb, tl, gw)
    svb_ref[...] = sv.astype(BF16).reshape(tb, tl, gw)
    for bi, rows in enumerate(seqs):
        skt_ref[bi] = kn[rows].T.astype(BF16)
        for h in range(SB_HEADS):
            hs = slice(h * SB_HEAD_DIM, (h + 1) * SB_HEAD_DIM)
            sk_ref[bi, h] = kn[rows, hs]
            sv_ref[bi, h] = sv[rows, hs]


def _front(x, mod, pos0, conv_prev, dn_conv_prev, pool_prev, k_all, v_all, p, layer):
    b, l, d = x.shape
    tb, tl = _row_tiles(b, l)
    gw, gw3 = GROUP_W, 3 * GROUP_W
    cprev = jnp.pad(conv_prev, ((0, 0), (CONV_PAD - (CONV_W - 1), 0), (0, 0)))
    pprev = jnp.pad(pool_prev, ((0, 0), (POOL_PAD - POOL_PREFIX, 0), (0, 0)))
    dprev = jnp.pad(dn_conv_prev, ((0, 0), (DN_PAD - (DN_CONV - 1), 0), (0, 0)))
    alog = jnp.pad(p["dn_a_log"], (0, AB_W - DN_HEADS)).reshape(1, AB_W)
    dtb = jnp.pad(p["dn_dt_bias"], (0, AB_W - DN_HEADS)).reshape(1, AB_W)
    qg = jnp.tile(p["sb_q_norm"], (1, SB_HEADS))
    kg = jnp.tile(p["sb_k_norm"], (1, SB_HEADS))
    assert k_all.shape[1:] == (b, SB_HEADS, l, SB_HEAD_DIM) and v_all.shape == k_all.shape
    rows = lambda n: pl.BlockSpec((tb, tl, n), lambda i, t: (i, t, 0))
    per_seq = lambda r, n: pl.BlockSpec((tb, r, n), lambda i, t: (i, 0, 0))
    cache = pl.BlockSpec((None, tb, SB_HEADS, tl, SB_HEAD_DIM), lambda i, t: (layer, i, 0, t, 0))
    f32_rows = jax.ShapeDtypeStruct((b, l, gw), F32)
    bf_rows = jax.ShapeDtypeStruct((b, l, gw), BF16)
    out = pl.pallas_call(
        functools.partial(_front_kernel, pos0=pos0),
        grid=(b // tb, l // tl),
        in_specs=[rows(d), per_seq(N_MOD, d), _const_spec((1, d)), _const_spec((d, sum(IN_WIDTHS)), layer),
                  per_seq(CONV_PAD, gw), _const_spec((CONV_W, gw)), _const_spec((1, gw)), _const_spec((1, gw)),
                  _const_spec((1, gw)),
                  per_seq(POOL_PAD, gw), _const_spec((gw, gw)), _const_spec((1, gw)),
                  per_seq(DN_PAD, gw3), _const_spec((DN_CONV, gw3)), _const_spec((1, AB_W)), _const_spec((1, AB_W)),
                  _const_spec((1, gw)), _const_spec((1, gw)),
                  pl.BlockSpec(memory_space=pl.ANY), pl.BlockSpec(memory_space=pl.ANY)],
        out_specs=[rows(gw), per_seq(CONV_PAD, gw), rows(gw), per_seq(POOL_PAD, gw),
                   rows(gw), rows(gw), rows(gw), rows(gw), rows(gw), rows(gw), per_seq(DN_PAD, gw3),
                   rows(gw), pl.BlockSpec((tb, gw, tl), lambda i, t: (i, 0, t)), rows(gw), cache, cache],
        out_shape=[bf_rows, jax.ShapeDtypeStruct((b, CONV_PAD, gw), F32),
                   bf_rows, jax.ShapeDtypeStruct((b, POOL_PAD, gw), F32),
                   f32_rows, f32_rows, f32_rows, f32_rows, f32_rows, f32_rows,
                   jax.ShapeDtypeStruct((b, DN_PAD, gw3), F32),
                   bf_rows, jax.ShapeDtypeStruct((b, gw, l), BF16), bf_rows,
                   jax.ShapeDtypeStruct(k_all.shape, F32), jax.ShapeDtypeStruct(v_all.shape, F32)],
        input_output_aliases={18: 14, 19: 15},
        scratch_shapes=[pltpu.VMEM((tb, CONV_PAD + tl, gw), F32),
                        pltpu.VMEM((V7X_SUBLANES - 1, CONV_PAD - V7X_SUBLANES + tl, gw), F32),
                        pltpu.VMEM((tb, POOL_PAD + tl, gw), F32),
                        pltpu.VMEM((tb, DN_PAD + tl, gw3), F32)],
        compiler_params=_cparams("arbitrary", "arbitrary"),
        name="front",
    )(x, mod, p["norm_mix"], p["w_in"],
      cprev, p["conv_dw_w"], p["conv_dw_b"], p["conv_ln_g"], p["conv_ln_b"],
      pprev, p["pool_w"], p["pool_scale"],
      dprev, p["dn_conv_w"], alog, dtb, qg, kg, k_all, v_all)
    (y_conv, ctail, y_pool, ptail, dq, dk, dv, dg, db, gate, dtail, sq, skt, svb, k_all, v_all) = out
    tails = (ctail[:, CONV_PAD - (CONV_W - 1):], dtail[:, DN_PAD - (DN_CONV - 1):], ptail[:, POOL_PAD - POOL_PREFIX:])
    return y_conv, y_pool, (dq, dk, dv, dg, db, gate), (sq, skt, svb), tails, k_all, v_all


def _unit_lower_inverses(lmats):
    c = lmats[0].shape[0]
    eye = (lax.broadcasted_iota(jnp.int32, (c, c), 0) == lax.broadcasted_iota(jnp.int32, (c, c), 1)).astype(F32)
    xs = [eye - m for m in lmats]
    ps = []
    for m in lmats:
        mh, ml = _split(m, 2)
        r1 = _dot(jnp.concatenate([mh, ml], axis=0), mh)
        ps.append(r1[0:c] + r1[c:2 * c] + _dot(mh, ml))
    span = 2
    while True:
        span *= 2
        last = span >= c
        nxt_x, nxt_p = [], []
        for x, p in zip(xs, ps):
            xh, xl = _split(x, 2)
            ph, pl_ = _split(p, 2)
            if last:
                r1 = _dot(jnp.concatenate([xh, xl], axis=0), ph)
                nxt_x.append(x + (r1[0:c] + r1[c:2 * c] + _dot(xh, pl_)))
            else:
                r1 = _dot(jnp.concatenate([xh, xl, ph, pl_], axis=0), ph)
                r2 = _dot(jnp.concatenate([xh, ph], axis=0), pl_)
                nxt_x.append(x + (r1[0:c] + r1[c:2 * c] + r2[0:c]))
                nxt_p.append(r1[2 * c:3 * c] + r1[3 * c:4 * c] + r2[c:2 * c])
        xs, ps = nxt_x, nxt_p
        if last:
            return xs


def _dn_kernel(q_ref, k_ref, v_ref, g_ref, b_ref, gate_ref, s0_ref, ng_ref, y_ref, s_ref, o_scr):
    tl = q_ref.shape[1]
    gw = GROUP_W
    hd = DN_HEAD_DIM

    @pl.when(pl.program_id(1) == 0)
    def _():
        s_ref[0] = s0_ref[0]

    ones_bd = _head_ones(gw, BF16)
    ci = lax.broadcasted_iota(jnp.int32, (CHUNK, CHUNK), 0)
    cj = lax.broadcasted_iota(jnp.int32, (CHUNK, CHUNK), 1)
    causal = ci >= cj
    strict = ci > cj
    tri = causal.astype(BF16)
    n_sub = min(DN_UNROLL, tl // CHUNK)
    heads = [slice(h * hd, (h + 1) * hd) for h in range(DN_HEADS)]

    def chunk_group(c, carry):
        pre = []
        lmats, attns = [], []
        for s in range(n_sub):
            rows = pl.ds(pl.multiple_of((c * n_sub + s) * CHUNK, CHUNK), CHUNK)
            qc, kc, vc, bc = q_ref[0, rows, :], k_ref[0, rows, :], v_ref[0, rows, :], b_ref[0, rows, :]
            gcb = _dot_sel_lhs(tri, g_ref[0, rows, :], 3)
            eg = jnp.exp(gcb)
            glast = gcb[CHUNK - 1:CHUNK, :]
            kb = kc * bc
            pre.append(dict(rows=rows, qc=qc, kc=kc, kb=kb, qd=qc * eg, ku=kc * jnp.exp(glast - gcb),
                            wr=kb * eg, vb=vc * bc, eglast=jnp.exp(glast)))
            for hs in heads:
                gh = gcb[:, hs]
                diff = gh - gh.T
                decay = jnp.where(causal, jnp.exp(jnp.where(causal, diff, 0.0)), 0.0)
                both = _dot_nt(jnp.concatenate([kb[:, hs], qc[:, hs]], axis=0).astype(BF16), kc[:, hs].astype(BF16))
                lmats.append(jnp.where(strict, both[0:CHUNK] * decay, 0.0))
                attns.append((both[CHUNK:2 * CHUNK] * decay).astype(BF16))
        tinvs = _unit_lower_inverses(lmats)
        sols = []
        for s in range(n_sub):
            p = pre[s]
            for h, hs in enumerate(heads):
                th, tlo = _split(tinvs[s * DN_HEADS + h], 2)
                rh, rlo = _split(jnp.concatenate([p["vb"][:, hs], p["wr"][:, hs]], axis=1), 2)
                r1 = _dot(jnp.concatenate([th, tlo], axis=0), rh)
                sols.append(r1[0:CHUNK] + r1[CHUNK:2 * CHUNK] + _dot(th, rlo))
        states = [s_ref[0, h] for h in range(DN_HEADS)]
        for s in range(n_sub):
            p = pre[s]
            outs = []
            for h, hs in enumerate(heads):
                i = s * DN_HEADS + h
                ws = _dot(jnp.concatenate([sols[i][:, hd:], p["qd"][:, hs]], axis=0).astype(BF16), states[h].astype(BF16))
                v_new = (sols[i][:, :hd] - ws[0:CHUNK]).astype(BF16)
                outs.append(ws[CHUNK:2 * CHUNK] + _dot(attns[i], v_new))
                states[h] = states[h] * p["eglast"][:, hs] + _dot_tn(p["ku"][:, hs].astype(BF16), v_new)
            o_scr[p["rows"], :] = jnp.concatenate(outs, axis=1)
        for h in range(DN_HEADS):
            s_ref[0, h] = states[h]
        return carry

    lax.fori_loop(0, tl // (CHUNK * n_sub), chunk_group, 0)

    o = o_scr[...]
    ms = _dot_sel_rhs(o * o, ones_bd, SQ_PARTS) * (1.0 / hd)
    gate = gate_ref[0]
    y_ref[0] = (o * lax.rsqrt(ms + EPS) * ng_ref[...] * (gate * _sigmoid(gate))).astype(y_ref.dtype)


def _dn_mixer(q, k, v, g, beta, gate, s0, norm_g):
    b, l, _ = q.shape
    assert l % CHUNK == 0
    tl = min(l, ROW_TILE)
    assert (tl // CHUNK) % min(DN_UNROLL, tl // CHUNK) == 0
    ng_p = jnp.tile(norm_g, DN_HEADS).reshape(1, GROUP_W)
    rows = pl.BlockSpec((1, tl, GROUP_W), lambda i, t: (i, t, 0))
    state_spec = pl.BlockSpec((1, DN_HEADS, DN_HEAD_DIM, DN_HEAD_DIM), lambda i, t: (i, 0, 0, 0))
    return pl.pallas_call(
        _dn_kernel,
        grid=(b, l // tl),
        in_specs=[rows] * 6 + [state_spec, _const_spec((1, GROUP_W))],
        out_specs=[rows, state_spec],
        out_shape=[jax.ShapeDtypeStruct((b, l, GROUP_W), BF16),
                   jax.ShapeDtypeStruct((b, DN_HEADS, DN_HEAD_DIM, DN_HEAD_DIM), F32)],
        scratch_shapes=[pltpu.VMEM((tl, GROUP_W), F32)],
        compiler_params=_cparams("arbitrary", "arbitrary"),
        name="deltanet_mixer",
    )(q, k, v, g, beta, gate, s0, ng_p)


def _sb_kernel(q_ref, kt_ref, v_ref, o_ref, later_scr, acc_scr, *, q_off, qb, kb):
    gw = GROUP_W
    q0 = q_off + pl.program_id(1) * qb
    nk = (q0 + qb - 1) // kb + 1
    head_of_lane = lax.broadcasted_iota(jnp.int32, (qb, gw), 1) // SB_HEAD_DIM
    q = q_ref[0].astype(F32)
    qh = [jnp.where(head_of_lane == h, q, 0.0).astype(BF16) for h in range(SB_HEADS)]
    head_of_vlane = lax.broadcasted_iota(jnp.int32, (kb, gw), 1) // SB_HEAD_DIM
    vmask = [(head_of_vlane == h).astype(BF16) for h in range(SB_HEADS)]
    r = lax.broadcasted_iota(jnp.int32, (kb, kb), 0)
    c = lax.broadcasted_iota(jnp.int32, (kb, kb), 1)
    suffix = (r >= c).astype(BF16)
    qpos = q0 + lax.broadcasted_iota(jnp.int32, (qb, kb), 0)
    kcol = lax.broadcasted_iota(jnp.int32, (qb, kb), 1)
    later_scr[...] = jnp.zeros_like(later_scr)
    acc_scr[...] = jnp.zeros_like(acc_scr)
    sign = jnp.uint32(0x80000000)

    def block(j):
        return pl.ds(pl.multiple_of(j * kb, kb), kb)

    def scores(js, masked=False):
        zs = [_dot(qh[h], kt_ref[0, :, block(j)]) for j in js for h in range(SB_HEADS)]
        if masked:
            zs = [jnp.where(js[i // SB_HEADS] * kb + kcol < qpos, z, SB_NEG) for i, z in enumerate(zs)]
        return zs

    def weights(zs):
        css = []
        for z in zs:
            neg_abs = lax.bitcast_convert_type(lax.bitcast_convert_type(z, jnp.uint32) | sign, F32)
            sp = jnp.maximum(z, 0.0) + jnp.log(1.0 + jnp.exp2(neg_abs)) * LOG2_E
            css.append(_dot(sp.astype(BF16), suffix))
        laters = [later_scr[h] for h in range(SB_HEADS)]
        parts = []
        for i, (z, cs) in enumerate(zip(zs, css)):
            h = i % SB_HEADS
            wide = jnp.concatenate([laters[h]] * (kb // V7X_LANES), axis=1)
            parts.append(jnp.exp2(z - cs - wide).astype(BF16))
            laters[h] = laters[h] + jnp.broadcast_to(cs[:, 0:1], laters[h].shape)
        for h in range(SB_HEADS):
            later_scr[h] = laters[h]
        return jnp.concatenate(parts, axis=1)

    def values(js):
        return jnp.concatenate([v_ref[0, block(j), :] * vmask[h] for j in js for h in range(SB_HEADS)], axis=0)

    def direct(js, masked):
        w = weights(scores(js, masked))
        halves = [js] if len(js) == 1 else [js[:len(js) // 2], js[len(js) // 2:]]
        lo = 0
        for part, sub in enumerate(halves):
            hi = lo + len(sub) * SB_HEADS * kb
            acc_scr[part] += _dot(w[:, lo:hi], values(sub))
            lo = hi

    direct([nk - 1], True)
    rest = nk - 1
    j = nk - 2
    width = 1
    while width < SB_UNROLL:
        take = (rest % SB_UNROLL) & width

        @pl.when(take != 0)
        def _(j=j, width=width):
            direct([j - s for s in range(width)], False)

        j = j - take
        width *= 2

    def body(i, carry):
        direct([j - SB_UNROLL * i - s for s in range(SB_UNROLL)], False)
        return carry

    lax.fori_loop(0, rest // SB_UNROLL, body, 0)
    o_ref[0] = (acc_scr[0] + acc_scr[1]).astype(o_ref.dtype)


def _sb_attention(q, kt_all, v_all, q_off):
    b, lq, gw = q.shape
    lk = v_all.shape[1]
    qb = min(SB_BLOCK, lq)
    kb = SB_BLOCK
    assert lq % qb == 0 and lk % kb == 0 and q_off % kb == 0 and kb % qb == 0 and SB_UNROLL & (SB_UNROLL - 1) == 0
    assert lk >= q_off + lq
    return pl.pallas_call(
        functools.partial(_sb_kernel, q_off=q_off, qb=qb, kb=kb),
        grid=(b, lq // qb),
        in_specs=[pl.BlockSpec((1, qb, gw), lambda i, t: (i, t, 0)),
                  pl.BlockSpec((1, gw, lk), lambda i, t: (i, 0, 0)),
                  pl.BlockSpec((1, lk, gw), lambda i, t: (i, 0, 0))],
        out_specs=pl.BlockSpec((1, qb, gw), lambda i, t: (i, t, 0)),
        out_shape=jax.ShapeDtypeStruct((b, lq, gw), BF16),
        scratch_shapes=[pltpu.VMEM((SB_HEADS, qb, V7X_LANES), F32), pltpu.VMEM((2, qb, gw), F32)],
        compiler_params=_cparams("arbitrary", "arbitrary"),
        name="sb_attention",
    )(q, kt_all, v_all)


def _sb_mixer(q, kt, vb, k_prev, v_prev):
    b, heads, past, d = k_prev.shape
    lq = q.shape[1]
    pad = (-(past + lq)) % SB_BLOCK
    if past or pad:
        kt_prev = jnp.swapaxes(k_prev, 2, 3).reshape(b, heads * d, past).astype(BF16)
        v_prev_rows = jnp.swapaxes(v_prev, 1, 2).reshape(b, past, heads * d).astype(BF16)
        kt = jnp.pad(jnp.concatenate([kt_prev, kt], axis=2), ((0, 0), (0, 0), (0, pad)))
        vb = jnp.pad(jnp.concatenate([v_prev_rows, vb], axis=1), ((0, 0), (0, pad), (0, 0)))
    return _sb_attention(q, kt, vb, past)


def _out_ffn_kernel(x_ref, yc_ref, yd_ref, yp_ref, ys_ref, mod_ref, g_ref, wo_ref, wg_ref, wu_ref, wd_ref, o_ref):
    tb, tl, d = x_ref.shape
    m = tb * tl
    gw = GROUP_W
    ycat = jnp.concatenate([r[...].reshape(m, gw) for r in (yc_ref, yd_ref, yp_ref, ys_ref)], axis=1)
    mix = _dot(ycat, wo_ref[...])
    x1 = x_ref[...] + mod_ref[:, 2:3, :] * mix.reshape(tb, tl, d)
    y = x1 * lax.rsqrt(jnp.mean(x1 * x1, axis=-1, keepdims=True) + EPS) * g_ref[...]
    h2 = (y * (1.0 + mod_ref[:, 4:5, :]) + mod_ref[:, 3:4, :]).reshape(m, d).astype(BF16)
    ff = D_FF // FF_SPLIT
    acc = jnp.zeros((m, d), F32)
    for c in range(FF_SPLIT):
        gate = _dot(h2, wg_ref[:, c * ff:(c + 1) * ff])
        up = _dot(h2, wu_ref[:, c * ff:(c + 1) * ff])
        act = (gate * _sigmoid(gate) * up).astype(BF16)
        acc = acc + _dot(act, wd_ref[c * ff:(c + 1) * ff, :])
    o_ref[...] = x1 + mod_ref[:, 5:6, :] * acc.reshape(tb, tl, d)


def _out_ffn(x, y_conv, y_dn, y_pool, y_sb, mod, norm_g, w_out, w_gate, w_up, w_down, layer):
    b, l, d = x.shape
    tb, tl = _row_tiles(b, l)
    assert (D_FF // FF_SPLIT) % V7X_LANES == 0
    row_spec = lambda n: pl.BlockSpec((tb, tl, n), lambda i, t: (i, t, 0))
    return pl.pallas_call(
        _out_ffn_kernel,
        grid=(b // tb, l // tl),
        in_specs=[row_spec(d), row_spec(GROUP_W), row_spec(GROUP_W), row_spec(GROUP_W), row_spec(GROUP_W),
                  pl.BlockSpec((tb, N_MOD, d), lambda i, t: (i, 0, 0)),
                  _const_spec((1, d)),
                  _const_spec((d, d), layer), _const_spec((d, D_FF), layer),
                  _const_spec((d, D_FF), layer), _const_spec((D_FF, d), layer)],
        out_specs=row_spec(d),
        out_shape=jax.ShapeDtypeStruct((b, l, d), F32),
        compiler_params=_cparams("arbitrary", "arbitrary"),
        name="out_ffn",
    )(x, y_conv, y_dn, y_pool, y_sb, mod, norm_g, w_out, w_gate, w_up, w_down)


def _pack_w_in(w_in):
    c0 = 2 * GROUP_W + 3 * GROUP_W + GROUP_W
    ab = w_in[..., c0:c0 + 2 * DN_HEADS]
    ab = jnp.pad(ab, ((0, 0), (0, 0), (0, AB_W - 2 * DN_HEADS)))
    return jnp.concatenate([w_in[..., :c0], ab, w_in[..., c0 + 2 * DN_HEADS:]], axis=-1).astype(BF16)


def _pool_block_diag(pool_w):
    g, n, _ = pool_w.shape
    out = jnp.zeros((g * n, g * n), pool_w.dtype)
    for i in range(g):
        out = out.at[i * n:(i + 1) * n, i * n:(i + 1) * n].set(pool_w[i])
    return out.astype(BF16)


def _trunk_layer(x, mod, pos0, conv_prev, dn_s0, dn_conv_prev, pool_prev, k_prev, v_prev, k_all, v_all, p, layer):
    y_conv, y_pool, dn_in, sb_in, (conv_new, dn_conv_new, pool_new), k_all, v_all = _front(
        x, mod, pos0, conv_prev, dn_conv_prev, pool_prev, k_all, v_all, p, layer)
    y_dn, s_new = _dn_mixer(*dn_in, dn_s0, p["dn_norm_g"])
    y_sb = _sb_mixer(*sb_in, k_prev, v_prev)
    x = _out_ffn(x, y_conv, y_dn, y_pool, y_sb, mod, p["norm_ffn"], p["w_out"], p["ffn_w_gate"],
                 p["ffn_w_up"], p["ffn_w_down"], layer)
    return x, conv_new, s_new, dn_conv_new, pool_new, k_all, v_all


def kernel(x_prompt, x_sample, c_prompt, c_sample, cache_conv, state_dn, cache_dn_conv, cache_pool, cache_sb_k, cache_sb_v, w_ada, b_ada, norm_mix, norm_ffn, w_in, w_out, conv_dw_w, conv_dw_b, conv_ln_g, conv_ln_b, dn_conv_w, dn_a_log, dn_dt_bias, dn_norm_g, pool_w, pool_scale, sb_q_norm, sb_k_norm, ffn_w_gate, ffn_w_up, ffn_w_down):
    depth = w_ada.shape[0]
    bp = x_prompt.shape[0]
    bs = x_sample.shape[0]
    past = cache_sb_k.shape[3]
    dt = x_prompt.dtype

    mod_all = _ada_modulation(jnp.concatenate([c_prompt, c_sample], axis=0), w_ada, b_ada)
    mod_all = mod_all.reshape(depth, bp + bs, N_MOD, D_MODEL)
    w_in_p = _pack_w_in(w_in)
    w_out_b, w_gate_b, w_up_b, w_down_b = (w.astype(BF16) for w in (w_out, ffn_w_gate, ffn_w_up, ffn_w_down))

    zeros_p = dict(
        conv=jnp.zeros((bp, CONV_W - 1, GROUP_W), dt),
        s0=jnp.zeros((bp, DN_HEADS, DN_HEAD_DIM, DN_HEAD_DIM), F32),
        dn_conv=jnp.zeros((bp, DN_CONV - 1, 3 * GROUP_W), dt),
        pool=jnp.zeros((bp, POOL_PREFIX, GROUP_W), dt),
        kv=jnp.zeros((bp, SB_HEADS, 0, SB_HEAD_DIM), dt))

    xp, xs = x_prompt, x_sample
    new_p = [[] for _ in range(4)]
    new_s = [[] for _ in range(4)]
    cache_buf = lambda x, name: _unwritten((depth, x.shape[0], SB_HEADS, x.shape[1], SB_HEAD_DIM), F32, name)
    sbk_p, sbv_p = cache_buf(xp, "k_prompt"), cache_buf(xp, "v_prompt")
    sbk_s, sbv_s = cache_buf(xs, "k_sample"), cache_buf(xs, "v_sample")
    for l in range(depth):
        row = lambda a: a[l].reshape(1, -1)
        p = dict(norm_mix=row(norm_mix), norm_ffn=row(norm_ffn), w_in=w_in_p, w_out=w_out_b,
                 conv_dw_w=conv_dw_w[l], conv_dw_b=row(conv_dw_b), conv_ln_g=row(conv_ln_g), conv_ln_b=row(conv_ln_b),
                 dn_conv_w=dn_conv_w[l], dn_a_log=dn_a_log[l], dn_dt_bias=dn_dt_bias[l], dn_norm_g=dn_norm_g[l],
                 pool_w=_pool_block_diag(pool_w[l]), pool_scale=row(pool_scale),
                 sb_q_norm=row(sb_q_norm), sb_k_norm=row(sb_k_norm),
                 ffn_w_gate=w_gate_b, ffn_w_up=w_up_b, ffn_w_down=w_down_b)
        xp, *sp, sbk_p, sbv_p = _trunk_layer(xp, mod_all[l, :bp], 0, zeros_p["conv"], zeros_p["s0"], zeros_p["dn_conv"],
                                             zeros_p["pool"], zeros_p["kv"], zeros_p["kv"], sbk_p, sbv_p, p, l)
        xs, *ss, sbk_s, sbv_s = _trunk_layer(xs, mod_all[l, bp:], past, cache_conv[l], state_dn[l], cache_dn_conv[l],
                                             cache_pool[l], cache_sb_k[l], cache_sb_v[l], sbk_s, sbv_s, p, l)
        for i in range(4):
            new_p[i].append(sp[i])
            new_s[i].append(ss[i])
    conv_p, dn_p, dnconv_p, pool_p = [jnp.stack(a) for a in new_p]
    conv_s, dn_s, dnconv_s, pool_s = [jnp.stack(a) for a in new_s]
    return (xp, xs, conv_p, conv_s, dn_p, dn_s, dnconv_p, dnconv_s, pool_p, pool_s, sbk_p, sbk_s, sbv_p, sbv_s)
```
